```python
import math
import jax, jax.numpy as jnp
from jax import lax
import numpy as np

D_MODEL = 1024
BATCH = 16
SEQ = 2048
DEPTH = 2

GRID_W = 64
CTX_LEN = 256
N_MIXERS = 2
POOL_WINDOWS = (2, 4, 8, 16)
POOL_GROUP = D_MODEL // len(POOL_WINDOWS)
DN_HEADS = 8
DN_DK = D_MODEL // DN_HEADS
DN_DV = D_MODEL // DN_HEADS
DN_CONV = 5
DN_CHUNK = 64
DN_HK = DN_HEADS * DN_DK
DN_HV = DN_HEADS * DN_DV
DN_PROJ = 2 * DN_HK + 2 * DN_HV + 4 * DN_HEADS
N_GROUPS = 4
EXPERTS_PER_GROUP = 4
N_EXPERTS = N_GROUPS * EXPERTS_PER_GROUP
TOP_K = 2
D_EXPERT = D_MODEL // 4
DEEPNORM_ALPHA = (2.0 * DEPTH) ** 0.25
DEEPNORM_BETA = (8.0 * DEPTH) ** -0.25
LN_EPS = 1e-5
N_POOL_LAYERS = (DEPTH + 1) // 2
N_DN_LAYERS = DEPTH // 2

kernel_name = "hybrid_pool_deltanet_moe_prefix_dit"


def layer_norm(x, g, b):
    xf = x.astype(jnp.float32)
    mu = jnp.mean(xf, -1, keepdims=True)
    var = jnp.mean(jnp.square(xf - mu), -1, keepdims=True)
    return ((xf - mu) * lax.rsqrt(var + LN_EPS) * g.astype(jnp.float32) + b.astype(jnp.float32)).astype(x.dtype)


def modulation(cvec, w_mod, b_mod):
    m = jax.nn.silu(cvec) @ w_mod + b_mod
    return jnp.split(m, 6, axis=-1)


def modulate(x, shift, scale):
    return x * (1 + scale) + shift


def post_norm(x, y, gate, g, b):
    return layer_norm(DEEPNORM_ALPHA * x + gate * y, g, b)


def box_mean(x, k, axis):
    n = x.shape[axis]
    cs = jnp.cumsum(x.astype(jnp.float32), axis=axis)
    zero = jnp.zeros_like(lax.slice_in_dim(cs, 0, 1, axis=axis))
    cs = jnp.concatenate([zero, cs], axis=axis)
    t = jnp.arange(n)
    lo = jnp.clip(t - k // 2, 0, n)
    hi = jnp.clip(t + k - k // 2, 0, n)
    s = jnp.take(cs, hi, axis=axis) - jnp.take(cs, lo, axis=axis)
    shape = [1] * x.ndim
    shape[axis] = n
    cnt = (hi - lo).astype(jnp.float32).reshape(shape)
    return (s / cnt).astype(x.dtype)


def pool_mixer(h, w_pool, scale, rows):
    B, L, D = h.shape
    outs = []
    for gi, k in enumerate(POOL_WINDOWS):
        hg = h[..., gi * POOL_GROUP:(gi + 1) * POOL_GROUP]
        if rows is None:
            m = box_mean(hg, k, 1)
        else:
            hg2 = hg.reshape(B, rows, GRID_W, POOL_GROUP)
            m = box_mean(box_mean(hg2, k, 1), k, 2).reshape(B, L, POOL_GROUP)
        outs.append(m - hg)
    p = jnp.stack(outs, axis=2)
    y = jnp.einsum('blgc,gcd->blgd', p, w_pool).reshape(B, L, D)
    return y * scale


def short_conv(u, w):
    K, C = w.shape
    pad = K // 2
    out = lax.conv_general_dilated(u, w.astype(u.dtype)[:, None, :], window_strides=(1,),
                                   padding=[(pad, K - 1 - pad)],
                                   dimension_numbers=('NWC', 'WIO', 'NWC'),
                                   feature_group_count=C)
    return jax.nn.silu(out)


def l2norm(t):
    return t * lax.rsqrt(jnp.sum(t * t, -1, keepdims=True) + 1e-6)


def gdn_chunked(q, k, v, g, beta, s0):
    B, H, L, dk = q.shape
    dv = v.shape[-1]
    C = DN_CHUNK
    N = L // C
    q = q.reshape(B, H, N, C, dk)
    k = k.reshape(B, H, N, C, dk)
    v = v.reshape(B, H, N, C, dv)
    g = jnp.cumsum(g.reshape(B, H, N, C), -1)
    beta = beta.reshape(B, H, N, C)
    tril = jnp.tril(jnp.ones((C, C), bool))
    strict = jnp.tril(jnp.ones((C, C), bool), -1)
    decay = jnp.exp(jnp.where(tril, g[..., :, None] - g[..., None, :], -jnp.inf))
    kb = k * beta[..., None]
    A = jnp.where(strict, jnp.einsum('bhnid,bhnjd->bhnij', kb, k) * decay, 0.0)
    eye = jnp.eye(C, dtype=A.dtype)
    rhs = jnp.concatenate([v * beta[..., None], kb * jnp.exp(g)[..., None]], -1)
    sol = lax.linalg.triangular_solve(eye + A, rhs, left_side=True, lower=True, unit_diagonal=True)
    u, w = sol[..., :dv], sol[..., dv:]
    qk = jnp.einsum('bhnid,bhnjd->bhnij', q, k) * decay
    q_dec = q * jnp.exp(g)[..., None]
    k_dec = k * jnp.exp(g[..., -1:] - g)[..., None]
    g_last = jnp.exp(g[..., -1])

    def step(S, xs):
        u_n, w_n, qk_n, qd_n, kd_n, gl_n = xs
        v_new = u_n - jnp.einsum('bhcd,bhde->bhce', w_n, S)
        o = jnp.einsum('bhcd,bhde->bhce', qd_n, S) + jnp.einsum('bhij,bhje->bhie', qk_n, v_new)
        S = S * gl_n[..., None, None] + jnp.einsum('bhcd,bhce->bhde', kd_n, v_new)
        return S, o

    xs = tuple(jnp.moveaxis(t, 2, 0) for t in (u, w, qk, q_dec, k_dec, g_last))
    S, o = lax.scan(step, s0, xs)
    o = jnp.moveaxis(o, 0, 2).reshape(B, H, L, dv)
    return o, S


def gdn_mixer(h, w_in, conv_w, a_log, dt_bias, norm_g, w_out, init_f, init_b):
    B, L, _ = h.shape
    f32 = jnp.float32
    proj = h @ w_in
    qkv = short_conv(proj[..., :2 * DN_HK + DN_HV], conv_w)
    z = proj[..., 2 * DN_HK + DN_HV:2 * DN_HK + 2 * DN_HV]
    o0 = 2 * DN_HK + 2 * DN_HV
    a = proj[..., o0:o0 + 2 * DN_HEADS].reshape(B, L, 2, DN_HEADS).astype(f32)
    b = proj[..., o0 + 2 * DN_HEADS:].reshape(B, L, 2, DN_HEADS).astype(f32)

    def heads(t, d):
        return t.reshape(B, L, DN_HEADS, d).transpose(0, 2, 1, 3).astype(f32)

    q = l2norm(heads(qkv[..., :DN_HK], DN_DK)) * (DN_DK ** -0.5)
    k = l2norm(heads(qkv[..., DN_HK:2 * DN_HK], DN_DK))
    v = heads(qkv[..., 2 * DN_HK:], DN_DV)
    g = -jnp.exp(a_log.astype(f32)) * jax.nn.softplus(a + dt_bias.astype(f32))
    g = g.transpose(2, 0, 3, 1)
    beta = jax.nn.sigmoid(b).transpose(2, 0, 3, 1)
    o_f, s_f = gdn_chunked(q, k, v, g[0], beta[0], init_f)
    rev = lambda t: jnp.flip(t, axis=2)
    o_b, s_b = gdn_chunked(rev(q), rev(k), rev(v), rev(g[1]), rev(beta[1]), init_b)
    o = o_f + rev(o_b)
    o = o * lax.rsqrt(jnp.mean(o * o, -1, keepdims=True) + 1e-6) * norm_g.astype(f32)
    o = o * jax.nn.silu(heads(z, DN_DV))
    o = o.transpose(0, 2, 1, 3).reshape(B, L, DN_HV).astype(h.dtype)
    return o @ w_out, s_f, s_b


def moe(h, w_router, b_router, w_gate, w_up, w_down):
    scores = jax.nn.sigmoid((h @ w_router).astype(jnp.float32))
    sel = scores + b_router.astype(jnp.float32)
    grp = sel.reshape(*sel.shape[:-1], N_GROUPS, EXPERTS_PER_GROUP)
    grp_score = jnp.sum(lax.top_k(grp, TOP_K)[0], -1)
    g_idx = jnp.argmax(grp_score, -1)
    in_group = jnp.take_along_axis(grp, g_idx[..., None, None], axis=-2)[..., 0, :]
    _, local = lax.top_k(in_group, TOP_K)
    e_idx = g_idx[..., None] * EXPERTS_PER_GROUP + local
    wts = jnp.take_along_axis(scores, e_idx, -1)
    wts = wts / jnp.sum(wts, -1, keepdims=True)
    combine = jnp.sum(jax.nn.one_hot(e_idx, N_EXPERTS, dtype=jnp.float32) * wts[..., None], -2)
    hid = jax.nn.silu(jnp.einsum('bld,edf->blef', h, w_gate)) * jnp.einsum('bld,edf->blef', h, w_up)
    y = jnp.einsum('blef,efd->bld', hid * combine[..., None].astype(hid.dtype), w_down)
    return y.astype(h.dtype)


def setup_inputs(seed: int = 0) -> dict:
    key = jax.random.key(seed)
    ks = jax.random.split(key, 20)
    D, E, F = D_MODEL, N_EXPERTS, D_EXPERT
    nrm = jax.random.normal
    x = nrm(ks[0], (BATCH, SEQ, D), jnp.float32)
    c = nrm(ks[1], (BATCH, D), jnp.float32)
    ctx = nrm(ks[2], (BATCH, CTX_LEN, D), jnp.float32)
    c_ctx = nrm(ks[3], (D,), jnp.float32)
    w_mod = nrm(ks[4], (DEPTH, D, 6 * D), jnp.float32) * D ** -0.5
    b_mod = 0.02 * nrm(ks[5], (DEPTH, 6 * D), jnp.float32)
    ln_g = 1.0 + 0.02 * nrm(ks[6], (DEPTH, 2, D), jnp.float32)
    ln_b = 0.02 * nrm(ks[7], (DEPTH, 2, D), jnp.float32)
    pool_w = nrm(ks[8], (N_POOL_LAYERS, len(POOL_WINDOWS), POOL_GROUP, POOL_GROUP), jnp.float32) * POOL_GROUP ** -0.5 * DEEPNORM_BETA
    pool_scale = 1.0 + 0.02 * nrm(ks[9], (N_POOL_LAYERS, D), jnp.float32)
    col_scale = jnp.ones((DN_PROJ,), jnp.float32).at[2 * DN_HK:2 * DN_HK + DN_HV].set(DEEPNORM_BETA)
    dn_w_in = nrm(ks[10], (N_DN_LAYERS, D, DN_PROJ), jnp.float32) * D ** -0.5 * col_scale
    dn_conv = nrm(ks[11], (N_DN_LAYERS, DN_CONV, 2 * DN_HK + DN_HV), jnp.float32) * DN_CONV ** -0.5
    dn_a_log = jnp.log(jax.random.uniform(ks[12], (N_DN_LAYERS, 2, DN_HEADS), jnp.float32, 1.0, 16.0))
    dt = jnp.exp(jax.random.uniform(ks[13], (N_DN_LAYERS, 2, DN_HEADS), jnp.float32, math.log(1e-3), math.log(1e-1)))
    dn_dt_bias = dt + jnp.log(-jnp.expm1(-dt))
    dn_norm = 1.0 + 0.02 * nrm(ks[14], (N_DN_LAYERS, DN_DV), jnp.float32)
    dn_w_out = nrm(ks[15], (N_DN_LAYERS, DN_HV, D), jnp.float32) * DN_HV ** -0.5 * DEEPNORM_BETA
    w_router = nrm(ks[16], (D, E), jnp.float32) * D ** -0.5
    b_router = 0.01 * nrm(ks[17], (E,), jnp.float32)
    kg, ku = jax.random.split(ks[18])
    w_gate = nrm(kg, (DEPTH, E, D, F), jnp.float32) * D ** -0.5
    w_up = nrm(ku, (DEPTH, E, D, F), jnp.float32) * D ** -0.5
    w_down = nrm(ks[19], (DEPTH, E, F, D), jnp.float32) * F ** -0.5 * DEEPNORM_BETA
    return {"x": x, "c": c, "ctx": ctx, "c_ctx": c_ctx, "w_mod": w_mod, "b_mod": b_mod,
            "ln_g": ln_g, "ln_b": ln_b, "pool_w": pool_w, "pool_scale": pool_scale,
            "dn_w_in": dn_w_in, "dn_conv": dn_conv, "dn_a_log": dn_a_log, "dn_dt_bias": dn_dt_bias,
            "dn_norm": dn_norm, "dn_w_out": dn_w_out, "w_router": w_router, "b_router": b_router,
            "w_gate": w_gate, "w_up": w_up, "w_down": w_down}


def reference(x, c, ctx, c_ctx, w_mod, b_mod, ln_g, ln_b, pool_w, pool_scale,
              dn_w_in, dn_conv, dn_a_log, dn_dt_bias, dn_norm, dn_w_out,
              w_router, b_router, w_gate, w_up, w_down):
    rows = x.shape[1] // GRID_W
    B = x.shape[0]
    for i in range(DEPTH):
        last = i == DEPTH - 1
        j = i // N_MIXERS
        ml = [t[:, None, :] for t in modulation(c, w_mod[i], b_mod[i])]
        mc = modulation(c_ctx, w_mod[i], b_mod[i])
        hx = modulate(x, ml[0], ml[1])
        hc = modulate(ctx, mc[0], mc[1])
        if i % N_MIXERS == 0:
            yx = pool_mixer(hx, pool_w[j], pool_scale[j], rows)
            yc = None if last else pool_mixer(hc, pool_w[j], pool_scale[j], None)
        else:
            zeros = jnp.zeros((B, DN_HEADS, DN_DK, DN_DV), jnp.float32)
            yc, s_f, s_b = gdn_mixer(hc, dn_w_in[j], dn_conv[j], dn_a_log[j], dn_dt_bias[j],
                                     dn_norm[j], dn_w_out[j], zeros, zeros)
            yx, _, _ = gdn_mixer(hx, dn_w_in[j], dn_conv[j], dn_a_log[j], dn_dt_bias[j],
                                 dn_norm[j], dn_w_out[j], s_f, s_b)
        x = post_norm(x, yx, ml[2], ln_g[i, 0], ln_b[i, 0])
        x = post_norm(x, moe(modulate(x, ml[3], ml[4]), w_router, b_router, w_gate[i], w_up[i], w_down[i]),
                      ml[5], ln_g[i, 1], ln_b[i, 1])
        if not last:
            ctx = post_norm(ctx, yc, mc[2], ln_g[i, 0], ln_b[i, 0])
            ctx = post_norm(ctx, moe(modulate(ctx, mc[3], mc[4]), w_router, b_router, w_gate[i], w_up[i], w_down[i]),
                            mc[5], ln_g[i, 1], ln_b[i, 1])
    return x
```

```python
import functools

import numpy as np
import jax
import jax.numpy as jnp
from jax import lax
from jax.experimental import pallas as pl
from jax.experimental.pallas import tpu as pltpu

GRID_W = 64
POOL_WINDOWS = (2, 4, 8, 16)
DN_HEADS = 8
DN_CHUNK = 64
N_GROUPS = 4
EXPERTS_PER_GROUP = 4
N_EXPERTS = N_GROUPS * EXPERTS_PER_GROUP
DEPTH = 2
DEEPNORM_ALPHA = (2.0 * DEPTH) ** 0.25
LN_EPS = 1e-5

V7X_LANES = 128
V7X_VMEM_LIMIT_BYTES = 56 * 1024 * 1024

F32 = jnp.float32
BF16 = jnp.bfloat16
HIGHEST = lax.Precision.HIGHEST


def _params(n_grid_dims):
    return pltpu.CompilerParams(
        dimension_semantics=("arbitrary",) * n_grid_dims,
        vmem_limit_bytes=V7X_VMEM_LIMIT_BYTES,
    )


def _resident(shape):
    zeros = (0,) * len(shape)
    return pl.BlockSpec(shape, lambda *_: zeros, pipeline_mode=pl.Buffered(1))


def _silu(x):
    return x * jax.nn.sigmoid(x)


def _layer_norm(v, g, b):
    mu = jnp.mean(v, axis=-1, keepdims=True)
    cen = v - mu
    var = jnp.mean(cen * cen, axis=-1, keepdims=True)
    return cen * lax.rsqrt(var + LN_EPS) * g + b


def _mod_kernel(c_ref, w_ref, b_ref, o_ref):
    s = _silu(c_ref[...])
    o_ref[0] = jnp.dot(s, w_ref[0], precision=HIGHEST, preferred_element_type=F32) + b_ref[0]


def _modulation(cvecs, w_mod, b_mod):
    R, D = cvecs.shape
    depth, _, six_d = w_mod.shape
    tn = 1536
    out = pl.pallas_call(
        _mod_kernel,
        grid=(depth, six_d // tn),
        in_specs=[
            pl.BlockSpec((R, D), lambda i, n: (0, 0)),
            pl.BlockSpec((1, D, tn), lambda i, n: (i, 0, n)),
            pl.BlockSpec((1, 1, tn), lambda i, n: (i, 0, n)),
        ],
        out_specs=pl.BlockSpec((1, R, tn), lambda i, n: (i, 0, n)),
        out_shape=jax.ShapeDtypeStruct((depth, R, six_d), F32),
        compiler_params=_params(2),
    )(cvecs, w_mod, b_mod.reshape(depth, 1, six_d))
    return out.reshape(depth, R, 6, D)


def _pool_kernel(x_ref, mod_ref, band_ref, icnt_ref, w_ref, ps_ref, o_ref, *, grid_rows):
    L, Cg = x_ref.shape[1], x_ref.shape[2]
    slab = band_ref.shape[1]
    g = pl.program_id(1)

    def body(k):
        h = x_ref[0] * (1.0 + mod_ref[0, 1:2, :]) + mod_ref[0, 0:1, :]
        if grid_rows is None:
            s = h
        else:
            r3 = h.reshape(grid_rows, L // grid_rows, Cg)
            zpad = jnp.zeros((k // 2,) + r3.shape[1:], F32)
            win = jnp.concatenate([zpad, r3, zpad], axis=0)
            span = 1
            while span < k:
                win = win[:-span] + win[span:]
                span *= 2
            s = win[:grid_rows].reshape(L, Cg)
        band = band_ref[0]
        w = w_ref[0].astype(BF16)
        for i in range(L // slab):
            sl = slice(i * slab, (i + 1) * slab)
            s_i = s[sl]
            hi = s_i.astype(BF16)
            lo = (s_i - hi.astype(F32)).astype(BF16)
            tot = jnp.dot(band, hi, preferred_element_type=F32) + jnp.dot(band, lo, preferred_element_type=F32)
            ic = icnt_ref[0, sl, :]
            mean = tot * jnp.concatenate([ic] * (Cg // V7X_LANES), axis=-1)
            p = mean - h[sl]
            y = jnp.dot(p.astype(BF16), w, preferred_element_type=F32)
            o_ref[0, sl, :] = y * ps_ref[...]

    if grid_rows is None:
        body(None)
    else:
        for gi, k in enumerate(POOL_WINDOWS):
            pl.when(g == gi)(functools.partial(body, k))


def _pool_tables(L, grid_rows, slab):
    n_g = len(POOL_WINDOWS)
    band = np.zeros((n_g, slab, slab), np.float32)
    icnt = np.zeros((n_g, L), np.float32)
    t = np.arange(L)
    for gi, k in enumerate(POOL_WINDOWS):
        if grid_rows is None:
            pos, n, blk = np.arange(slab), L, np.zeros(slab, np.int64)
            cnt = np.clip(t + k - k // 2, 0, L) - np.clip(t - k // 2, 0, L)
        else:
            w = L // grid_rows
            pos, n, blk = np.arange(slab) % w, w, np.arange(slab) // w
            row, col = t // w, t % w
            cnt_r = np.clip(row + k - k // 2, 0, grid_rows) - np.clip(row - k // 2, 0, grid_rows)
            cnt_c = np.clip(col + k - k // 2, 0, w) - np.clip(col - k // 2, 0, w)
            cnt = cnt_r * cnt_c
        lo = np.clip(pos - k // 2, 0, n)
        hi = np.clip(pos + k - k // 2, 0, n)
        inside = (pos[None, :] >= lo[:, None]) & (pos[None, :] < hi[:, None]) & (blk[None, :] == blk[:, None])
        band[gi] = inside.astype(np.float32)
        icnt[gi] = 1.0 / cnt
    icnt = np.broadcast_to(icnt[:, :, None], (n_g, L, V7X_LANES))
    return jnp.asarray(band, BF16), jnp.asarray(icnt, F32)


def _pool_mixer(x, mod, mod_row, pool_w, pool_scale, grid_rows):
    B, L, D = x.shape
    n_g = len(POOL_WINDOWS)
    Cg = D // n_g
    slab = 256
    assert L % slab == 0 and (grid_rows is None or (L == slab or slab % (L // grid_rows) == 0))
    if grid_rows is None:
        assert L == slab
    band, icnt = _pool_tables(L, grid_rows, slab)
    return pl.pallas_call(
        functools.partial(_pool_kernel, grid_rows=grid_rows),
        grid=(B, n_g),
        in_specs=[
            pl.BlockSpec((1, L, Cg), lambda b, g: (b, 0, g)),
            pl.BlockSpec((1, 6, Cg), lambda b, g: (mod_row(b), 0, g)),
            pl.BlockSpec((1, slab, slab), lambda b, g: (g, 0, 0)),
            pl.BlockSpec((1, L, V7X_LANES), lambda b, g: (g, 0, 0)),
            pl.BlockSpec((1, Cg, Cg), lambda b, g: (g, 0, 0)),
            pl.BlockSpec((1, Cg), lambda b, g: (0, g)),
        ],
        out_specs=pl.BlockSpec((1, L, Cg), lambda b, g: (b, 0, g)),
        out_shape=jax.ShapeDtypeStruct((B, L, D), F32),
        compiler_params=_params(2),
    )(x, mod, band, icnt, pool_w, pool_scale.reshape(1, D))


def _route(scores, bias):
    sel = [s + b for s, b in zip(scores, bias)]
    E = EXPERTS_PER_GROUP
    grp = []
    for gi in range(N_GROUPS):
        v = sel[gi * E:(gi + 1) * E]
        best = None
        for a in range(E):
            for c in range(a + 1, E):
                pair = v[a] + v[c]
                best = pair if best is None else jnp.maximum(best, pair)
        grp.append(best)
    g_idx = jnp.zeros_like(grp[0], dtype=jnp.int32)
    g_best = grp[0]
    for gi in range(1, N_GROUPS):
        better = grp[gi] > g_best
        g_idx = jnp.where(better, gi, g_idx)
        g_best = jnp.where(better, grp[gi], g_best)
    in_sel, in_score = [], []
    for l in range(E):
        vs, vc = sel[l], scores[l]
        for gi in range(1, N_GROUPS):
            vs = jnp.where(g_idx == gi, sel[gi * E + l], vs)
            vc = jnp.where(g_idx == gi, scores[gi * E + l], vc)
        in_sel.append(vs)
        in_score.append(vc)
    i1 = jnp.zeros_like(g_idx)
    m1 = in_sel[0]
    for l in range(1, E):
        better = in_sel[l] > m1
        i1 = jnp.where(better, l, i1)
        m1 = jnp.where(better, in_sel[l], m1)
    i2 = jnp.full_like(g_idx, -1)
    m2 = jnp.full_like(m1, -jnp.inf)
    for l in range(E):
        better = (i1 != l) & ((in_sel[l] > m2) | (i2 < 0))
        i2 = jnp.where(better, l, i2)
        m2 = jnp.where(better, in_sel[l], m2)
    s1 = in_score[0]
    s2 = in_score[0]
    for l in range(1, E):
        s1 = jnp.where(i1 == l, in_score[l], s1)
        s2 = jnp.where(i2 == l, in_score[l], s2)
    tot = s1 + s2
    w1, w2 = s1 / tot, s2 / tot
    e1 = g_idx * E + i1
    e2 = g_idx * E + i2
    return [jnp.where(e1 == e, w1, 0.0) + jnp.where(e2 == e, w2, 0.0) for e in range(N_EXPERTS)]


def _post_router_kernel(*refs, has_proj):
    if has_proj:
        x_ref, y_ref, wo_ref, mod_ref, ln_ref, wr_ref, br_ref, x1_ref, h_ref, cmb_ref = refs
        y = jnp.dot(y_ref[...].astype(BF16), wo_ref[...], preferred_element_type=F32)
    else:
        x_ref, y_ref, mod_ref, ln_ref, wr_ref, br_ref, x1_ref, h_ref, cmb_ref = refs
        y = y_ref[...]
    v = DEEPNORM_ALPHA * x_ref[...] + mod_ref[0, 2:3, :] * y
    x1 = _layer_norm(v, ln_ref[0:1, :], ln_ref[1:2, :])
    x1_ref[...] = x1
    h = x1 * (1.0 + mod_ref[0, 4:5, :]) + mod_ref[0, 3:4, :]
    h_ref[...] = h.astype(BF16)
    logits = lax.dot_general(wr_ref[...], h, (((1,), (1,)), ((), ())), precision=HIGHEST,
                             preferred_element_type=F32)
    scores = jax.nn.sigmoid(logits)
    T = scores.shape[1]
    bias = br_ref[...]
    rows = _route([scores[e:e + 1, :] for e in range(N_EXPERTS)],
                  [jnp.broadcast_to(bias[e:e + 1, :], (1, T)) for e in range(N_EXPERTS)])
    for e in range(N_EXPERTS):
        cmb_ref[e:e + 1, :] = rows[e]


def _post_router(x, y, w_out, mod, mod_row, ln, w_router_t, b_router, tile):
    n_tok, D = x.shape
    has_proj = w_out is not None
    tok = pl.BlockSpec((tile, D), lambda t: (t, 0))
    in_specs = [tok, tok]
    args = [x, y]
    if has_proj:
        in_specs.append(_resident(w_out.shape))
        args.append(w_out)
    in_specs += [
        pl.BlockSpec((1, 6, D), lambda t: (mod_row(t), 0, 0)),
        _resident(ln.shape),
        _resident(w_router_t.shape),
        _resident((N_EXPERTS, 1)),
    ]
    args += [mod, ln, w_router_t, b_router.reshape(N_EXPERTS, 1)]
    return pl.pallas_call(
        functools.partial(_post_router_kernel, has_proj=has_proj),
        grid=(n_tok // tile,),
        in_specs=in_specs,
        out_specs=[tok, tok, pl.BlockSpec((N_EXPERTS, tile), lambda t: (0, t))],
        out_shape=[
            jax.ShapeDtypeStruct((n_tok, D), F32),
            jax.ShapeDtypeStruct((n_tok, D), BF16),
            jax.ShapeDtypeStruct((N_EXPERTS, n_tok), F32),
        ],
        compiler_params=_params(1),
    )(*args)


def _moe_kernel(h_ref, cmb_ref, x1_ref, mod_ref, ln_ref, wgu_ref, wd_ref, o_ref, hid_ref):
    F = wgu_ref.shape[2] // 2
    h = h_ref[...]
    cmb = cmb_ref[...]
    for e in range(N_EXPERTS):
        gu = jnp.dot(h, wgu_ref[e], preferred_element_type=F32)
        hid = _silu(gu[:, :F]) * gu[:, F:] * cmb[:, e:e + 1]
        hid_ref[:, e * F:(e + 1) * F] = hid.astype(BF16)
    y = jnp.dot(hid_ref[...], wd_ref[...], preferred_element_type=F32)
    v = DEEPNORM_ALPHA * x1_ref[...] + mod_ref[0, 5:6, :] * y
    o_ref[...] = _layer_norm(v, ln_ref[0:1, :], ln_ref[1:2, :])


def _moe_post(h, cmb, x1, mod, mod_row, ln, w_gu, w_d, tile):
    n_tok, D = x1.shape
    E, _, F2 = w_gu.shape
    tok = pl.BlockSpec((tile, D), lambda t: (t, 0))
    return pl.pallas_call(
        _moe_kernel,
        grid=(n_tok // tile,),
        in_specs=[
            tok,
            pl.BlockSpec((tile, E), lambda t: (t, 0)),
            tok,
            pl.BlockSpec((1, 6, D), lambda t: (mod_row(t), 0, 0)),
            _resident(ln.shape),
            _resident(w_gu.shape),
            _resident(w_d.shape),
        ],
        out_specs=tok,
        out_shape=jax.ShapeDtypeStruct((n_tok, D), F32),
        scratch_shapes=[pltpu.VMEM((tile, E * F2 // 2), BF16)],
        compiler_params=_params(1),
    )(h, cmb, x1, mod, ln, w_gu, w_d)


def _proj_kernel(x_ref, mod_ref, w_ref, wab_ref, alog_ref, dtb_ref, cum_ref, p_ref, gb_ref):
    T = x_ref.shape[0]
    H2 = alog_ref.shape[1]
    h = x_ref[...] * (1.0 + mod_ref[0, 1:2, :]) + mod_ref[0, 0:1, :]
    p_ref[...] = jnp.dot(h.astype(BF16), w_ref[...], preferred_element_type=F32)
    ab = jnp.dot(h, wab_ref[...], precision=HIGHEST, preferred_element_type=F32)
    a = ab[:, :H2] + dtb_ref[...]
    softplus = jnp.maximum(a, 0.0) + jnp.log1p(jnp.exp(-jnp.abs(a)))
    g = -jnp.exp(alog_ref[...]) * softplus
    beta = jax.nn.sigmoid(ab[:, H2:])
    sums = jnp.dot(cum_ref[...], g, precision=HIGHEST, preferred_element_type=F32)
    lane = lax.broadcasted_iota(jnp.int32, (T, H2), 1)
    gc = jnp.where(lane < H2 // 2, sums[:T], sums[T:2 * T])
    gb_ref[...] = jnp.concatenate([gc, beta, sums[2 * T:], jnp.zeros_like(gc)], axis=-1)


def _cum_table(tile, chunk):
    t = np.arange(tile)
    same = (t[:, None] // chunk) == (t[None, :] // chunk)
    lower = same & (t[None, :] <= t[:, None])
    upper = same & (t[None, :] >= t[:, None])
    return jnp.asarray(np.concatenate([lower, upper, same], axis=0).astype(np.float32))


def _gdn_proj(x, mod, mod_row, w_main, w_ab, a_log, dt_bias, tile):
    n_tok, D = x.shape
    P = w_main.shape[1]
    H2 = a_log.size
    return pl.pallas_call(
        _proj_kernel,
        grid=(n_tok // tile,),
        in_specs=[
            pl.BlockSpec((tile, D), lambda t: (t, 0)),
            pl.BlockSpec((1, 6, D), lambda t: (mod_row(t), 0, 0)),
            _resident(w_main.shape),
            _resident(w_ab.shape),
            _resident((1, H2)),
            _resident((1, H2)),
            _resident((3 * tile, tile)),
        ],
        out_specs=[pl.BlockSpec((tile, P), lambda t: (t, 0)), pl.BlockSpec((tile, 4 * H2), lambda t: (t, 0))],
        out_shape=[jax.ShapeDtypeStruct((n_tok, P), F32), jax.ShapeDtypeStruct((n_tok, 4 * H2), F32)],
        compiler_params=_params(1),
    )(x, mod, w_main, w_ab, a_log.reshape(1, H2), dt_bias.reshape(1, H2), _cum_table(tile, DN_CHUNK))


def _gdn_kernel(q_ref, k_ref, v_ref, z_ref, cq_ref, ck_ref, cv_ref, gcol_ref, grow_ref, s0f_ref, s0b_ref, ng_ref,
                o_ref, sf_ref, sb_ref,
                qn_scr, kn_scr, vn_scr, wq_scr, u_scr, qk_scr, kd_scr, gl_scr, o_scr, *, group):
    L, dk = q_ref.shape[1], q_ref.shape[2]
    C = DN_CHUNK
    N = L // C
    n_taps = 5

    def conv_silu(u, w):
        rows = lax.broadcasted_iota(jnp.int32, (L, 1), 0)
        acc = u * w[n_taps // 2:n_taps // 2 + 1, :]
        for j in range(n_taps):
            d = j - n_taps // 2
            if d == 0:
                continue
            shifted = pltpu.roll(u, (-d) % L, axis=0)
            valid = (rows + d >= 0) & (rows + d < L)
            acc = acc + jnp.where(valid, shifted, 0.0) * w[j:j + 1, :]
        return _silu(acc)

    q = conv_silu(q_ref[0], cq_ref[...])
    k = conv_silu(k_ref[0], ck_ref[...])
    qn_scr[...] = q * lax.rsqrt(jnp.sum(q * q, axis=-1, keepdims=True) + 1e-6) * (dk ** -0.5)
    kn_scr[...] = k * lax.rsqrt(jnp.sum(k * k, axis=-1, keepdims=True) + 1e-6)
    vn_scr[...] = conv_silu(v_ref[0], cv_ref[...])

    ii = lax.broadcasted_iota(jnp.int32, (C, C), 0)
    jj = lax.broadcasted_iota(jnp.int32, (C, C), 1)
    eye = (ii == jj).astype(F32)

    def bmm(a, b):
        return jnp.einsum('nij,njd->nid', a.astype(BF16), b.astype(BF16), preferred_element_type=F32)

    def bmm_nt(a, b):
        return jnp.einsum('nid,njd->nij', a.astype(BF16), b.astype(BF16), preferred_element_type=F32)

    def local(gi, _):
        c0 = gi * group
        tok = pl.ds(pl.multiple_of(c0 * C, group * C), group * C)
        q3 = qn_scr[tok, :].reshape(group, C, dk)
        k3 = kn_scr[tok, :].reshape(group, C, dk)
        v3 = vn_scr[tok, :].reshape(group, C, dk)
        gcol = gcol_ref[0, 0, tok, :]
        for d in range(2):
            incl = (ii >= jj) if d == 0 else (ii <= jj)
            strict = (ii > jj) if d == 0 else (ii < jj)
            gc = gcol[:, d:d + 1].reshape(group, C, 1)
            beta = gcol[:, 2 + d:3 + d].reshape(group, C, 1)
            gt = gcol[:, 4 + d:5 + d].reshape(group, C, 1)
            gcr = grow_ref[0, 0, d, pl.ds(c0, group), :]
            decay = jnp.exp(jnp.where(incl[None], gc - gcr[:, None, :], -jnp.inf))
            kb = k3 * beta
            A = jnp.where(strict[None], bmm_nt(kb, k3) * decay, 0.0)
            hi_idx, lo_idx = (ii, jj) if d == 0 else (jj, ii)
            s = 1
            X = eye[None]
            while s < C:
                off = ((ii & -(2 * s)) == (jj & -(2 * s))) & ((hi_idx & s) != 0) & ((lo_idx & s) == 0)
                a_off = jnp.where(off[None], A, 0.0)
                X = X - (a_off if s == 1 else bmm(X, bmm(a_off, X)))
                s *= 2
            eg = jnp.exp(gc)
            sol = bmm(X, jnp.concatenate([v3 * beta, kb * eg], axis=-1))
            u_scr[d, pl.ds(c0, group)] = sol[..., :dk]
            wq_scr[d, pl.ds(c0, group)] = jnp.concatenate([sol[..., dk:], q3 * eg], axis=1).astype(BF16)
            qk_scr[d, pl.ds(c0, group)] = jnp.where(incl[None], bmm_nt(q3, k3) * decay, 0.0).astype(BF16)
            kd_scr[d, pl.ds(c0, group)] = (k3 * jnp.exp(gt - gc)).astype(BF16)
            gl_scr[d, pl.ds(c0, group)] = jnp.broadcast_to(jnp.exp(gt[:, 0:1, :]), (group, 8, dk))
        return 0

    lax.fori_loop(0, N // group, local, 0)

    def chain(d, n, S):
        r = jnp.dot(wq_scr[d, n], S.astype(BF16), preferred_element_type=F32)
        v_new = (u_scr[d, n] - r[:C]).astype(BF16)
        o = r[C:] + jnp.dot(qk_scr[d, n], v_new, preferred_element_type=F32)
        upd = lax.dot_general(kd_scr[d, n], v_new, (((0,), (0,)), ((), ())), preferred_element_type=F32)
        return o, S * gl_scr[d, n][0:1, :] + upd

    def step(i, carry):
        s_f, s_b = carry
        o_f, s_f = chain(0, i, s_f)
        o_b, s_b = chain(1, N - 1 - i, s_b)
        o_scr[0, i] = o_f
        o_scr[1, N - 1 - i] = o_b
        return s_f, s_b

    s_f, s_b = lax.fori_loop(0, N, step, (s0f_ref[0, 0], s0b_ref[0, 0]))
    sf_ref[0, 0] = s_f
    sb_ref[0, 0] = s_b

    o = (o_scr[0] + o_scr[1]).reshape(L, dk)
    o = o * lax.rsqrt(jnp.mean(o * o, axis=-1, keepdims=True) + 1e-6) * ng_ref[...]
    o_ref[0] = o * _silu(z_ref[0])


def _gdn_core(proj, conv_w, gcol, grow, s0f, s0b, norm_g):
    B, L, _ = proj.shape
    H = DN_HEADS
    dk = norm_g.shape[-1]
    C = DN_CHUNK
    N = L // C
    group = min(N, 8)
    col = lambda off: pl.BlockSpec((1, L, dk), lambda b, h: (b, 0, off * H + h))
    cw = lambda off: pl.BlockSpec((8, dk), lambda b, h: (0, off * H + h))
    st = pl.BlockSpec((1, 1, dk, dk), lambda b, h: (b, h, 0, 0))
    return pl.pallas_call(
        functools.partial(_gdn_kernel, group=group),
        grid=(B, H),
        in_specs=[
            col(0), col(1), col(2), col(3), cw(0), cw(1), cw(2),
            pl.BlockSpec((1, 1, L, 6), lambda b, h: (b, h, 0, 0)),
            pl.BlockSpec((1, 1, 2, N, C), lambda b, h: (b, h, 0, 0, 0)),
            st, st,
            pl.BlockSpec((1, dk), lambda b, h: (0, 0)),
        ],
        out_specs=[pl.BlockSpec((1, L, dk), lambda b, h: (b, 0, h)), st, st],
        out_shape=[
            jax.ShapeDtypeStruct((B, L, H * dk), F32),
            jax.ShapeDtypeStruct((B, H, dk, dk), F32),
            jax.ShapeDtypeStruct((B, H, dk, dk), F32),
        ],
        scratch_shapes=[
            pltpu.VMEM((L, dk), F32), pltpu.VMEM((L, dk), F32), pltpu.VMEM((L, dk), F32),
            pltpu.VMEM((2, N, 2 * C, dk), BF16),
            pltpu.VMEM((2, N, C, dk), F32),
            pltpu.VMEM((2, N, C, C), BF16),
            pltpu.VMEM((2, N, C, dk), BF16),
            pltpu.VMEM((2, N, 8, dk), F32),
            pltpu.VMEM((2, N, C, dk), F32),
        ],
        compiler_params=_params(2),
    )(proj, proj, proj, proj, conv_w, conv_w, conv_w, gcol, grow, s0f, s0b, norm_g.reshape(1, dk))


def _gdn_mixer(x_tok, B, mod, mod_row, w_main, w_ab, conv_w, a_log, dt_bias, norm_g, s0f, s0b, tile):
    n_tok = x_tok.shape[0]
    L = n_tok // B
    H, C = DN_HEADS, DN_CHUNK
    proj, gb = _gdn_proj(x_tok, mod, mod_row, w_main, w_ab, a_log, dt_bias, tile)
    gb = gb.reshape(B, L, 4, 2, H)[:, :, :3]
    gcol = gb.transpose(0, 4, 1, 2, 3).reshape(B, H, L, 6)
    grow = gb[:, :, 0].transpose(0, 3, 2, 1).reshape(B, H, 2, L // C, C)
    o, s_f, s_b = _gdn_core(proj.reshape(B, L, -1), conv_w, gcol, grow, s0f, s0b, norm_g)
    return o.reshape(n_tok, -1), s_f, s_b


def kernel(x, c, ctx, c_ctx, w_mod, b_mod, ln_g, ln_b, pool_w, pool_scale, dn_w_in, dn_conv, dn_a_log, dn_dt_bias, dn_norm, dn_w_out, w_router, b_router, w_gate, w_up, w_down):
    B, L, D = x.shape
    Lc = ctx.shape[1]
    E = w_router.shape[1]
    F = w_gate.shape[-1]
    H = DN_HEADS
    dk = dn_norm.shape[-1]
    hk = H * dk
    ctx_row = B

    n_rows = -(-(B + 1) // 8) * 8
    cvecs = jnp.concatenate([c, c_ctx[None, :], jnp.zeros((n_rows - B - 1, D), F32)], axis=0)
    mods = _modulation(cvecs, w_mod, b_mod)
    ln = jnp.stack([ln_g, ln_b], axis=2)
    w_router_t = w_router.T
    tile = 512
    x_row = lambda t: t // (L // tile)
    c_row = lambda t: ctx_row

    mod = mods[0]
    w_gu = jnp.concatenate([w_gate[0], w_up[0]], axis=-1).astype(BF16)
    w_d = w_down[0].reshape(E * F, D).astype(BF16)
    streams = []
    for tok, rows, row_of_batch, row_of_tile in ((x, L // GRID_W, lambda b: b, x_row), (ctx, None, lambda b: ctx_row, c_row)):
        y = _pool_mixer(tok, mod, row_of_batch, pool_w[0], pool_scale[0], rows)
        tok2 = tok.reshape(-1, D)
        x1, h, cmb_t = _post_router(tok2, y.reshape(-1, D), None, mod, row_of_tile, ln[0, 0], w_router_t, b_router, tile)
        streams.append(_moe_post(h, cmb_t.T, x1, mod, row_of_tile, ln[0, 1], w_gu, w_d, tile))
    x_tok, ctx_tok = streams

    mod = mods[1]
    w_in = dn_w_in[0]
    w_main = w_in[:, :4 * hk].astype(BF16)
    w_ab = w_in[:, 4 * hk:]
    conv_w = jnp.concatenate([dn_conv[0], jnp.zeros((8 - dn_conv.shape[1], 3 * hk), F32)], axis=0)
    a_log, dt_bias, norm_g = dn_a_log[0], dn_dt_bias[0], dn_norm[0]
    zeros = jnp.zeros((B, H, dk, dk), F32)
    _, s_f, s_b = _gdn_mixer(ctx_tok, B, mod, c_row, w_main, w_ab, conv_w, a_log, dt_bias, norm_g, zeros, zeros, tile)
    o, _, _ = _gdn_mixer(x_tok, B, mod, x_row, w_main, w_ab, conv_w, a_log, dt_bias, norm_g, s_f, s_b, tile)
    w_gu = jnp.concatenate([w_gate[1], w_up[1]], axis=-1).astype(BF16)
    w_d = w_down[1].reshape(E * F, D).astype(BF16)
    x1, h, cmb_t = _post_router(x_tok, o, dn_w_out[0].astype(BF16), mod, x_row, ln[1, 0], w_router_t, b_router, tile)
    out = _moe_post(h, cmb_t.T, x1, mod, x_row, ln[1, 1], w_gu, w_d, tile)
    return out.reshape(B, L, D)
```

```python
import functools

import numpy as np
import jax
import jax.numpy as jnp
from jax import lax
from jax.experimental import pallas as pl
from jax.experimental.pallas import tpu as pltpu

GRID_W = 64
POOL_WINDOWS = (2, 4, 8, 16)
DN_HEADS = 8
DN_CHUNK = 64
DN_CONV_TAPS = 5
N_GROUPS = 4
EXPERTS_PER_GROUP = 4
N_EXPERTS = N_GROUPS * EXPERTS_PER_GROUP
DEPTH = 2
DEEPNORM_ALPHA = (2.0 * DEPTH) ** 0.25
LN_EPS = 1e-5

V7X_LANES = 128
V7X_VMEM_LIMIT_BYTES = 56 * 1024 * 1024

GDN_GROUP_CHUNKS = 8
GDN_HEADS_PER_STEP = 4

F32 = jnp.float32
BF16 = jnp.bfloat16
HIGHEST = lax.Precision.HIGHEST


def _params(n_grid_dims):
    return pltpu.CompilerParams(
        dimension_semantics=("arbitrary",) * n_grid_dims,
        vmem_limit_bytes=V7X_VMEM_LIMIT_BYTES,
    )


def _resident(shape):
    zeros = (0,) * len(shape)
    return pl.BlockSpec(shape, lambda *_: zeros, pipeline_mode=pl.Buffered(1))


def _silu(x):
    return x * jax.nn.sigmoid(x)


def _layer_norm(v, g, b):
    mu = jnp.mean(v, axis=-1, keepdims=True)
    cen = v - mu
    var = jnp.mean(cen * cen, axis=-1, keepdims=True)
    return cen * lax.rsqrt(var + LN_EPS) * g + b


def _mod_kernel(c_ref, w_ref, b_ref, o_ref):
    s = _silu(c_ref[...])
    o_ref[0] = jnp.dot(s, w_ref[0], precision=HIGHEST, preferred_element_type=F32) + b_ref[0]


def _modulation(cvecs, w_mod, b_mod):
    R, D = cvecs.shape
    depth, _, six_d = w_mod.shape
    tn = 1536
    out = pl.pallas_call(
        _mod_kernel,
        grid=(depth, six_d // tn),
        in_specs=[
            pl.BlockSpec((R, D), lambda i, n: (0, 0)),
            pl.BlockSpec((1, D, tn), lambda i, n: (i, 0, n)),
            pl.BlockSpec((1, 1, tn), lambda i, n: (i, 0, n)),
        ],
        out_specs=pl.BlockSpec((1, R, tn), lambda i, n: (i, 0, n)),
        out_shape=jax.ShapeDtypeStruct((depth, R, six_d), F32),
        compiler_params=_params(2),
    )(cvecs, w_mod, b_mod.reshape(depth, 1, six_d))
    return out.reshape(depth, R, 6, D)


def _pool_kernel(x_ref, mod_ref, band_ref, icnt_ref, w_ref, ps_ref, o_ref, *, grid_rows):
    L, Cg = x_ref.shape[1], x_ref.shape[2]
    slab = band_ref.shape[1]
    g = pl.program_id(1)

    def body(k):
        h = x_ref[0] * (1.0 + mod_ref[0, 1:2, :]) + mod_ref[0, 0:1, :]
        if grid_rows is None:
            s = h
        else:
            r3 = h.reshape(grid_rows, L // grid_rows, Cg)
            zpad = jnp.zeros((k // 2,) + r3.shape[1:], F32)
            win = jnp.concatenate([zpad, r3, zpad], axis=0)
            span = 1
            while span < k:
                win = win[:-span] + win[span:]
                span *= 2
            s = win[:grid_rows].reshape(L, Cg)
        band = band_ref[0]
        w = w_ref[0].astype(BF16)
        for i in range(L // slab):
            sl = slice(i * slab, (i + 1) * slab)
            s_i = s[sl]
            hi = s_i.astype(BF16)
            lo = (s_i - hi.astype(F32)).astype(BF16)
            tot = jnp.dot(band, hi, preferred_element_type=F32) + jnp.dot(band, lo, preferred_element_type=F32)
            ic = icnt_ref[0, sl, :]
            mean = tot * jnp.concatenate([ic] * (Cg // V7X_LANES), axis=-1)
            p = mean - h[sl]
            y = jnp.dot(p.astype(BF16), w, preferred_element_type=F32)
            o_ref[0, sl, :] = y * ps_ref[...]

    if grid_rows is None:
        body(None)
    else:
        for gi, k in enumerate(POOL_WINDOWS):
            pl.when(g == gi)(functools.partial(body, k))


def _pool_tables(L, grid_rows, slab):
    n_g = len(POOL_WINDOWS)
    band = np.zeros((n_g, slab, slab), np.float32)
    icnt = np.zeros((n_g, L), np.float32)
    t = np.arange(L)
    for gi, k in enumerate(POOL_WINDOWS):
        if grid_rows is None:
            pos, n, blk = np.arange(slab), L, np.zeros(slab, np.int64)
            cnt = np.clip(t + k - k // 2, 0, L) - np.clip(t - k // 2, 0, L)
        else:
            w = L // grid_rows
            pos, n, blk = np.arange(slab) % w, w, np.arange(slab) // w
            row, col = t // w, t % w
            cnt_r = np.clip(row + k - k // 2, 0, grid_rows) - np.clip(row - k // 2, 0, grid_rows)
            cnt_c = np.clip(col + k - k // 2, 0, w) - np.clip(col - k // 2, 0, w)
            cnt = cnt_r * cnt_c
        lo = np.clip(pos - k // 2, 0, n)
        hi = np.clip(pos + k - k // 2, 0, n)
        inside = (pos[None, :] >= lo[:, None]) & (pos[None, :] < hi[:, None]) & (blk[None, :] == blk[:, None])
        band[gi] = inside.astype(np.float32)
        icnt[gi] = 1.0 / cnt
    icnt = np.broadcast_to(icnt[:, :, None], (n_g, L, V7X_LANES))
    return jnp.asarray(band, BF16), jnp.asarray(icnt, F32)


def _pool_mixer(x, mod, mod_row, pool_w, pool_scale, grid_rows):
    B, L, D = x.shape
    n_g = len(POOL_WINDOWS)
    Cg = D // n_g
    slab = 256
    assert L % slab == 0 and (grid_rows is None or (L == slab or slab % (L // grid_rows) == 0))
    if grid_rows is None:
        assert L == slab
    band, icnt = _pool_tables(L, grid_rows, slab)
    return pl.pallas_call(
        functools.partial(_pool_kernel, grid_rows=grid_rows),
        grid=(B, n_g),
        in_specs=[
            pl.BlockSpec((1, L, Cg), lambda b, g: (b, 0, g)),
            pl.BlockSpec((1, 6, Cg), lambda b, g: (mod_row(b), 0, g)),
            pl.BlockSpec((1, slab, slab), lambda b, g: (g, 0, 0)),
            pl.BlockSpec((1, L, V7X_LANES), lambda b, g: (g, 0, 0)),
            pl.BlockSpec((1, Cg, Cg), lambda b, g: (g, 0, 0)),
            pl.BlockSpec((1, Cg), lambda b, g: (0, g)),
        ],
        out_specs=pl.BlockSpec((1, L, Cg), lambda b, g: (b, 0, g)),
        out_shape=jax.ShapeDtypeStruct((B, L, D), F32),
        compiler_params=_params(2),
    )(x, mod, band, icnt, pool_w, pool_scale.reshape(1, D))


def _route(scores, bias):
    sel = [s + b for s, b in zip(scores, bias)]
    E = EXPERTS_PER_GROUP
    grp = []
    for gi in range(N_GROUPS):
        v = sel[gi * E:(gi + 1) * E]
        best = None
        for a in range(E):
            for c in range(a + 1, E):
                pair = v[a] + v[c]
                best = pair if best is None else jnp.maximum(best, pair)
        grp.append(best)
    g_idx = jnp.zeros_like(grp[0], dtype=jnp.int32)
    g_best = grp[0]
    for gi in range(1, N_GROUPS):
        better = grp[gi] > g_best
        g_idx = jnp.where(better, gi, g_idx)
        g_best = jnp.where(better, grp[gi], g_best)
    in_sel, in_score = [], []
    for l in range(E):
        vs, vc = sel[l], scores[l]
        for gi in range(1, N_GROUPS):
            vs = jnp.where(g_idx == gi, sel[gi * E + l], vs)
            vc = jnp.where(g_idx == gi, scores[gi * E + l], vc)
        in_sel.append(vs)
        in_score.append(vc)
    i1 = jnp.zeros_like(g_idx)
    m1 = in_sel[0]
    for l in range(1, E):
        better = in_sel[l] > m1
        i1 = jnp.where(better, l, i1)
        m1 = jnp.where(better, in_sel[l], m1)
    i2 = jnp.full_like(g_idx, -1)
    m2 = jnp.full_like(m1, -jnp.inf)
    for l in range(E):
        better = (i1 != l) & ((in_sel[l] > m2) | (i2 < 0))
        i2 = jnp.where(better, l, i2)
        m2 = jnp.where(better, in_sel[l], m2)
    s1 = in_score[0]
    s2 = in_score[0]
    for l in range(1, E):
        s1 = jnp.where(i1 == l, in_score[l], s1)
        s2 = jnp.where(i2 == l, in_score[l], s2)
    tot = s1 + s2
    w1, w2 = s1 / tot, s2 / tot
    e1 = g_idx * E + i1
    e2 = g_idx * E + i2
    return [jnp.where(e1 == e, w1, 0.0) + jnp.where(e2 == e, w2, 0.0) for e in range(N_EXPERTS)]


def _post_router_kernel(*refs, has_proj):
    if has_proj:
        x_ref, y_ref, wo_ref, mod_ref, ln_ref, wr_ref, br_ref, x1_ref, h_ref, cmb_ref = refs
        y = jnp.dot(y_ref[...].astype(BF16), wo_ref[...], preferred_element_type=F32)
    else:
        x_ref, y_ref, mod_ref, ln_ref, wr_ref, br_ref, x1_ref, h_ref, cmb_ref = refs
        y = y_ref[...]
    v = DEEPNORM_ALPHA * x_ref[...] + mod_ref[0, 2:3, :] * y
    x1 = _layer_norm(v, ln_ref[0:1, :], ln_ref[1:2, :])
    x1_ref[...] = x1
    h = x1 * (1.0 + mod_ref[0, 4:5, :]) + mod_ref[0, 3:4, :]
    h_ref[...] = h.astype(BF16)
    logits = lax.dot_general(wr_ref[...], h, (((1,), (1,)), ((), ())), precision=HIGHEST,
                             preferred_element_type=F32)
    scores = jax.nn.sigmoid(logits)
    T = scores.shape[1]
    bias = br_ref[...]
    rows = _route([scores[e:e + 1, :] for e in range(N_EXPERTS)],
                  [jnp.broadcast_to(bias[e:e + 1, :], (1, T)) for e in range(N_EXPERTS)])
    for e in range(N_EXPERTS):
        cmb_ref[e:e + 1, :] = rows[e]


def _post_router(x, y, w_out, mod, mod_row, ln, w_router_t, b_router, tile):
    n_tok, D = x.shape
    has_proj = w_out is not None
    tok = pl.BlockSpec((tile, D), lambda t: (t, 0))
    in_specs = [tok, tok]
    args = [x, y]
    if has_proj:
        in_specs.append(_resident(w_out.shape))
        args.append(w_out)
    in_specs += [
        pl.BlockSpec((1, 6, D), lambda t: (mod_row(t), 0, 0)),
        _resident(ln.shape),
        _resident(w_router_t.shape),
        _resident((N_EXPERTS, 1)),
    ]
    args += [mod, ln, w_router_t, b_router.reshape(N_EXPERTS, 1)]
    return pl.pallas_call(
        functools.partial(_post_router_kernel, has_proj=has_proj),
        grid=(n_tok // tile,),
        in_specs=in_specs,
        out_specs=[tok, tok, pl.BlockSpec((N_EXPERTS, tile), lambda t: (0, t))],
        out_shape=[
            jax.ShapeDtypeStruct((n_tok, D), F32),
            jax.ShapeDtypeStruct((n_tok, D), BF16),
            jax.ShapeDtypeStruct((N_EXPERTS, n_tok), F32),
        ],
        compiler_params=_params(1),
    )(*args)


def _moe_kernel(h_ref, cmb_ref, x1_ref, mod_ref, ln_ref, wgu_ref, wd_ref, o_ref, hid_ref):
    F = wgu_ref.shape[2] // 2
    h = h_ref[...]
    cmb = cmb_ref[...]
    for e in range(N_EXPERTS):
        gu = jnp.dot(h, wgu_ref[e], preferred_element_type=F32)
        hid = _silu(gu[:, :F]) * gu[:, F:] * cmb[:, e:e + 1]
        hid_ref[:, e * F:(e + 1) * F] = hid.astype(BF16)
    y = jnp.dot(hid_ref[...], wd_ref[...], preferred_element_type=F32)
    v = DEEPNORM_ALPHA * x1_ref[...] + mod_ref[0, 5:6, :] * y
    o_ref[...] = _layer_norm(v, ln_ref[0:1, :], ln_ref[1:2, :])


def _moe_post(h, cmb, x1, mod, mod_row, ln, w_gu, w_d, tile):
    n_tok, D = x1.shape
    E, _, F2 = w_gu.shape
    tok = pl.BlockSpec((tile, D), lambda t: (t, 0))
    return pl.pallas_call(
        _moe_kernel,
        grid=(n_tok // tile,),
        in_specs=[
            tok,
            pl.BlockSpec((tile, E), lambda t: (t, 0)),
            tok,
            pl.BlockSpec((1, 6, D), lambda t: (mod_row(t), 0, 0)),
            _resident(ln.shape),
            _resident(w_gu.shape),
            _resident(w_d.shape),
        ],
        out_specs=tok,
        out_shape=jax.ShapeDtypeStruct((n_tok, D), F32),
        scratch_shapes=[pltpu.VMEM((tile, E * F2 // 2), BF16)],
        compiler_params=_params(1),
    )(h, cmb, x1, mod, ln, w_gu, w_d)


def _proj_kernel(x_ref, xp_ref, xn_ref, mod_ref, wqkv_ref, wz_ref, wab_ref, cw_ref, alog_ref, dtb_ref, cum_ref,
                 q_ref, k_ref, v_ref, z_ref, gb_ref, *, tiles_per_seq):
    T = x_ref.shape[0]
    H, dk = q_ref.shape[1], q_ref.shape[3]
    hk = H * dk
    H2 = alog_ref.shape[1]
    halo = xp_ref.shape[0]
    n_taps = DN_CONV_TAPS
    scale = 1.0 + mod_ref[0, 1:2, :]
    shift = mod_ref[0, 0:1, :]
    h = x_ref[...] * scale + shift
    pos = pl.program_id(0) % tiles_per_seq
    prev_ok = (pos != 0).astype(F32)
    next_ok = (pos != tiles_per_seq - 1).astype(F32)
    h_ext = jnp.concatenate([xp_ref[...] * scale + shift, h, xn_ref[...] * scale + shift], axis=0)
    u = jnp.dot(h_ext.astype(BF16), wqkv_ref[...], preferred_element_type=F32)
    rows = lax.broadcasted_iota(jnp.int32, (T + 2 * halo, 1), 0)
    u = u * jnp.where(rows < halo, prev_ok, jnp.where(rows >= T + halo, next_ok, 1.0))
    acc = None
    for j in range(n_taps):
        start = halo - n_taps // 2 + j
        term = u[start:start + T] * cw_ref[j:j + 1, :]
        acc = term if acc is None else acc + term
    act = _silu(acc)
    for hd in range(H):
        qh = act[:, hd * dk:(hd + 1) * dk]
        kh = act[:, hk + hd * dk:hk + (hd + 1) * dk]
        q_ref[0, hd] = (qh * (lax.rsqrt(jnp.sum(qh * qh, axis=-1, keepdims=True) + 1e-6) * dk ** -0.5)).astype(BF16)
        k_ref[0, hd] = (kh * lax.rsqrt(jnp.sum(kh * kh, axis=-1, keepdims=True) + 1e-6)).astype(BF16)
        v_ref[0, hd] = act[:, 2 * hk + hd * dk:2 * hk + (hd + 1) * dk].astype(BF16)
    z = jnp.dot(h.astype(BF16), wz_ref[...], preferred_element_type=F32)
    for hd in range(H):
        z_ref[0, hd] = z[:, hd * dk:(hd + 1) * dk].astype(BF16)
    ab = jnp.dot(h, wab_ref[...], precision=HIGHEST, preferred_element_type=F32)
    a = ab[:, :H2] + dtb_ref[...]
    softplus = jnp.maximum(a, 0.0) + jnp.log1p(jnp.exp(-jnp.abs(a)))
    g = -jnp.exp(alog_ref[...]) * softplus
    beta = jax.nn.sigmoid(ab[:, H2:])
    sums = jnp.dot(cum_ref[...], g, precision=HIGHEST, preferred_element_type=F32)
    lane = lax.broadcasted_iota(jnp.int32, (T, H2), 1)
    gc = jnp.where(lane < H2 // 2, sums[:T], sums[T:2 * T])
    gb_ref[...] = jnp.concatenate([gc, beta, sums[2 * T:], jnp.zeros_like(gc)], axis=-1)


def _cum_table(tile, chunk):
    t = np.arange(tile)
    same = (t[:, None] // chunk) == (t[None, :] // chunk)
    lower = same & (t[None, :] <= t[:, None])
    upper = same & (t[None, :] >= t[:, None])
    return jnp.asarray(np.concatenate([lower, upper, same], axis=0).astype(np.float32))


def _gdn_proj(x, B, mod, mod_row, w_qkv, w_z, w_ab, conv_w, a_log, dt_bias, tile):
    n_tok, D = x.shape
    L = n_tok // B
    H = DN_HEADS
    dk = w_z.shape[1] // H
    H2 = a_log.size
    halo = 8
    tps = L // tile
    n_halo_blocks = n_tok // halo
    head_major = pl.BlockSpec((1, H, tile, dk), lambda t: (t // tps, 0, t % tps, 0))
    qkvz = jax.ShapeDtypeStruct((B, H, L, dk), BF16)
    return pl.pallas_call(
        functools.partial(_proj_kernel, tiles_per_seq=tps),
        grid=(n_tok // tile,),
        in_specs=[
            pl.BlockSpec((tile, D), lambda t: (t, 0)),
            pl.BlockSpec((halo, D), lambda t: (jnp.maximum(t * (tile // halo) - 1, 0), 0)),
            pl.BlockSpec((halo, D), lambda t: (jnp.minimum((t + 1) * (tile // halo), n_halo_blocks - 1), 0)),
            pl.BlockSpec((1, 6, D), lambda t: (mod_row(t), 0, 0)),
            _resident(w_qkv.shape),
            _resident(w_z.shape),
            _resident(w_ab.shape),
            _resident(conv_w.shape),
            _resident((1, H2)),
            _resident((1, H2)),
            _resident((3 * tile, tile)),
        ],
        out_specs=[head_major, head_major, head_major, head_major, pl.BlockSpec((tile, 4 * H2), lambda t: (t, 0))],
        out_shape=[qkvz, qkvz, qkvz, qkvz, jax.ShapeDtypeStruct((n_tok, 4 * H2), F32)],
        compiler_params=_params(1),
    )(x, x, x, mod, w_qkv, w_z, w_ab, conv_w, a_log.reshape(1, H2), dt_bias.reshape(1, H2),
      _cum_table(tile, DN_CHUNK))


def _gdn_kernel(q_ref, k_ref, v_ref, z_ref, gcol_ref, grow_ref, s0f_ref, s0b_ref, ng_ref,
                o_ref, sf_ref, sb_ref,
                s_scr, wq_scr, u_scr, qk_scr, kd_scr, gl_scr, oacc_scr, *, group):
    Hb, L, dk = q_ref.shape[1], q_ref.shape[2], q_ref.shape[3]
    C = DN_CHUNK
    N = L // C
    n_groups = N // group
    gtok = group * C

    for hh in range(Hb):
        s_scr[2 * hh] = s0f_ref[0, hh]
        s_scr[2 * hh + 1] = s0b_ref[0, hh]
    oacc_scr[...] = jnp.zeros(oacc_scr.shape, F32)

    ii = lax.broadcasted_iota(jnp.int32, (C, C), 0)
    jj = lax.broadcasted_iota(jnp.int32, (C, C), 1)
    eye = (ii == jj).astype(F32)

    def bmm(a, b):
        return jnp.einsum('nij,njd->nid', a.astype(BF16), b.astype(BF16), preferred_element_type=F32)

    def bmm_nt(a, b):
        return jnp.einsum('nid,njd->nij', a.astype(BF16), b.astype(BF16), preferred_element_type=F32)

    def local(hh, d, cg):
        slot = 2 * hh + d
        tok = pl.ds(pl.multiple_of(cg * gtok, gtok), gtok)
        q3 = q_ref[0, hh, tok, :].astype(F32).reshape(group, C, dk)
        k3 = k_ref[0, hh, tok, :].astype(F32).reshape(group, C, dk)
        v3 = v_ref[0, hh, tok, :].astype(F32).reshape(group, C, dk)
        gcol = gcol_ref[0, hh, tok, :]
        incl = (ii >= jj) if d == 0 else (ii <= jj)
        strict = (ii > jj) if d == 0 else (ii < jj)
        gc = gcol[:, d:d + 1].reshape(group, C, 1)
        beta = gcol[:, 2 + d:3 + d].reshape(group, C, 1)
        gt = gcol[:, 4 + d:5 + d].reshape(group, C, 1)
        gcr = grow_ref[0, hh, d, pl.ds(pl.multiple_of(cg * group, group), group), :]
        decay = jnp.exp(jnp.where(incl[None], gc - gcr[:, None, :], -jnp.inf))
        kb = k3 * beta
        A = jnp.where(strict[None], bmm_nt(kb, k3) * decay, 0.0)
        hi_idx, lo_idx = (ii, jj) if d == 0 else (jj, ii)
        s = 1
        X = eye[None]
        while s < C:
            off = ((ii & -(2 * s)) == (jj & -(2 * s))) & ((hi_idx & s) != 0) & ((lo_idx & s) == 0)
            a_off = jnp.where(off[None], A, 0.0)
            X = X - (a_off if s == 1 else bmm(X, bmm(a_off, X)))
            s *= 2
        eg = jnp.exp(gc)
        sol = bmm(X, jnp.concatenate([v3 * beta, kb * eg], axis=-1))
        u_scr[slot] = sol[..., :dk]
        wq_scr[slot] = jnp.concatenate([sol[..., dk:], q3 * eg], axis=1).astype(BF16)
        qk_scr[slot] = jnp.where(incl[None], bmm_nt(q3, k3) * decay, 0.0).astype(BF16)
        kd_scr[slot] = (k3 * jnp.exp(gt - gc)).astype(BF16)
        gl_scr[slot] = jnp.broadcast_to(jnp.exp(gt[:, 0:1, :]), (group, 8, dk))

    def chain(hh, d, cg, n):
        slot = 2 * hh + d
        S = s_scr[slot]
        r = jnp.dot(wq_scr[slot, n], S.astype(BF16), preferred_element_type=F32)
        v_new = (u_scr[slot, n] - r[:C]).astype(BF16)
        o = r[C:] + jnp.dot(qk_scr[slot, n], v_new, preferred_element_type=F32)
        upd = lax.dot_general(kd_scr[slot, n], v_new, (((0,), (0,)), ((), ())), preferred_element_type=F32)
        s_scr[slot] = S * gl_scr[slot, n][0:1, :] + upd
        rows = pl.ds(pl.multiple_of((cg * group + n) * C, C), C)
        oacc_scr[hh, rows, :] += o

    def group_body(gi, _):
        cgs = (gi, n_groups - 1 - gi)

        def per_head(hh, _):
            for d in range(2):
                local(hh, d, cgs[d])
            return 0

        lax.fori_loop(0, Hb, per_head, 0)

        def step(i, _):
            for hh in range(Hb):
                chain(hh, 0, cgs[0], i)
                chain(hh, 1, cgs[1], group - 1 - i)
            return 0

        lax.fori_loop(0, group, step, 0)
        return 0

    lax.fori_loop(0, n_groups, group_body, 0)

    for hh in range(Hb):
        sf_ref[0, hh] = s_scr[2 * hh]
        sb_ref[0, hh] = s_scr[2 * hh + 1]
        o = oacc_scr[hh]
        o = o * lax.rsqrt(jnp.mean(o * o, axis=-1, keepdims=True) + 1e-6) * ng_ref[...]
        o_ref[0, :, hh * dk:(hh + 1) * dk] = (o * _silu(z_ref[0, hh].astype(F32))).astype(BF16)


def _gdn_core(q, k, v, z, gcol, grow, s0f, s0b, norm_g):
    B, H, L, dk = q.shape
    C = DN_CHUNK
    N = L // C
    group = min(N, GDN_GROUP_CHUNKS)
    Hb = GDN_HEADS_PER_STEP
    n_chains = 2 * Hb
    seq = pl.BlockSpec((1, Hb, L, dk), lambda b, h: (b, h, 0, 0))
    st = pl.BlockSpec((1, Hb, dk, dk), lambda b, h: (b, h, 0, 0))
    return pl.pallas_call(
        functools.partial(_gdn_kernel, group=group),
        grid=(B, H // Hb),
        in_specs=[
            seq, seq, seq, seq,
            pl.BlockSpec((1, Hb, L, 6), lambda b, h: (b, h, 0, 0)),
            pl.BlockSpec((1, Hb, 2, N, C), lambda b, h: (b, h, 0, 0, 0)),
            st, st,
            pl.BlockSpec((1, dk), lambda b, h: (0, 0)),
        ],
        out_specs=[pl.BlockSpec((1, L, Hb * dk), lambda b, h: (b, 0, h)), st, st],
        out_shape=[
            jax.ShapeDtypeStruct((B, L, H * dk), BF16),
            jax.ShapeDtypeStruct((B, H, dk, dk), F32),
            jax.ShapeDtypeStruct((B, H, dk, dk), F32),
        ],
        scratch_shapes=[
            pltpu.VMEM((n_chains, dk, dk), F32),
            pltpu.VMEM((n_chains, group, 2 * C, dk), BF16),
            pltpu.VMEM((n_chains, group, C, dk), F32),
            pltpu.VMEM((n_chains, group, C, C), BF16),
            pltpu.VMEM((n_chains, group, C, dk), BF16),
            pltpu.VMEM((n_chains, group, 8, dk), F32),
            pltpu.VMEM((Hb, L, dk), F32),
        ],
        compiler_params=_params(2),
    )(q, k, v, z, gcol, grow, s0f, s0b, norm_g.reshape(1, dk))


def _gdn_mixer(x_tok, B, mod, mod_row, w_qkv, w_z, w_ab, conv_w, a_log, dt_bias, norm_g, s0f, s0b, tile):
    n_tok = x_tok.shape[0]
    L = n_tok // B
    H, C = DN_HEADS, DN_CHUNK
    q, k, v, z, gb = _gdn_proj(x_tok, B, mod, mod_row, w_qkv, w_z, w_ab, conv_w, a_log, dt_bias, tile)
    gb = gb.reshape(B, L, 4, 2, H)[:, :, :3]
    gcol = gb.transpose(0, 4, 1, 2, 3).reshape(B, H, L, 6)
    grow = gb[:, :, 0].transpose(0, 3, 2, 1).reshape(B, H, 2, L // C, C)
    o, s_f, s_b = _gdn_core(q, k, v, z, gcol, grow, s0f, s0b, norm_g)
    return o.reshape(n_tok, -1), s_f, s_b


def kernel(x, c, ctx, c_ctx, w_mod, b_mod, ln_g, ln_b, pool_w, pool_scale, dn_w_in, dn_conv, dn_a_log, dn_dt_bias, dn_norm, dn_w_out, w_router, b_router, w_gate, w_up, w_down):
    B, L, D = x.shape
    Lc = ctx.shape[1]
    E = w_router.shape[1]
    F = w_gate.shape[-1]
    H = DN_HEADS
    dk = dn_norm.shape[-1]
    hk = H * dk
    ctx_row = B

    n_rows = -(-(B + 1) // 8) * 8
    cvecs = jnp.concatenate([c, c_ctx[None, :], jnp.zeros((n_rows - B - 1, D), F32)], axis=0)
    mods = _modulation(cvecs, w_mod, b_mod)
    ln = jnp.stack([ln_g, ln_b], axis=2)
    w_router_t = w_router.T
    tile = 512
    x_row = lambda t: t // (L // tile)
    c_row = lambda t: ctx_row

    mod = mods[0]
    w_gu = jnp.concatenate([w_gate[0], w_up[0]], axis=-1).astype(BF16)
    w_d = w_down[0].reshape(E * F, D).astype(BF16)
    streams = []
    for tok, rows, row_of_batch, row_of_tile in ((x, L // GRID_W, lambda b: b, x_row), (ctx, None, lambda b: ctx_row, c_row)):
        y = _pool_mixer(tok, mod, row_of_batch, pool_w[0], pool_scale[0], rows)
        tok2 = tok.reshape(-1, D)
        x1, h, cmb_t = _post_router(tok2, y.reshape(-1, D), None, mod, row_of_tile, ln[0, 0], w_router_t, b_router, tile)
        streams.append(_moe_post(h, cmb_t.T, x1, mod, row_of_tile, ln[0, 1], w_gu, w_d, tile))
    x_tok, ctx_tok = streams

    mod = mods[1]
    w_in = dn_w_in[0]
    w_qkv = w_in[:, :3 * hk].astype(BF16)
    w_z = w_in[:, 3 * hk:4 * hk].astype(BF16)
    w_ab = w_in[:, 4 * hk:]
    conv_w = jnp.concatenate([dn_conv[0], jnp.zeros((8 - dn_conv.shape[1], 3 * hk), F32)], axis=0)
    a_log, dt_bias, norm_g = dn_a_log[0], dn_dt_bias[0], dn_norm[0]
    zeros = jnp.zeros((B, H, dk, dk), F32)
    _, s_f, s_b = _gdn_mixer(ctx_tok, B, mod, c_row, w_qkv, w_z, w_ab, conv_w, a_log, dt_bias, norm_g, zeros, zeros,
                             min(tile, Lc))
    o, _, _ = _gdn_mixer(x_tok, B, mod, x_row, w_qkv, w_z, w_ab, conv_w, a_log, dt_bias, norm_g, s_f, s_b, tile)
    w_gu = jnp.concatenate([w_gate[1], w_up[1]], axis=-1).astype(BF16)
    w_d = w_down[1].reshape(E * F, D).astype(BF16)
    x1, h, cmb_t = _post_router(x_tok, o, dn_w_out[0].astype(BF16), mod, x_row, ln[1, 0], w_router_t, b_router, tile)
    out = _moe_post(h, cmb_t.T, x1, mod, x_row, ln[1, 1], w_gu, w_d, tile)
    return out.reshape(B, L, D)
```

```python
import functools

import numpy as np
import jax
import jax.numpy as jnp
from jax import lax
from jax.experimental import pallas as pl
from jax.experimental.pallas import tpu as pltpu

GRID_W = 64
POOL_WINDOWS = (2, 4, 8, 16)
DN_HEADS = 8
DN_CHUNK = 64
DN_CONV_TAPS = 5
N_GROUPS = 4
EXPERTS_PER_GROUP = 4
N_EXPERTS = N_GROUPS * EXPERTS_PER_GROUP
DEPTH = 2
DEEPNORM_ALPHA = (2.0 * DEPTH) ** 0.25
LN_EPS = 1e-5

V7X_LANES = 128
V7X_VMEM_LIMIT_BYTES = 56 * 1024 * 1024

GDN_GROUP_CHUNKS = 8
GDN_HEADS_PER_STEP = 4

F32 = jnp.float32
BF16 = jnp.bfloat16
HIGHEST = lax.Precision.HIGHEST


def _params(n_grid_dims):
    return pltpu.CompilerParams(
        dimension_semantics=("arbitrary",) * n_grid_dims,
        vmem_limit_bytes=V7X_VMEM_LIMIT_BYTES,
    )


def _resident(shape):
    zeros = (0,) * len(shape)
    return pl.BlockSpec(shape, lambda *_: zeros, pipeline_mode=pl.Buffered(1))


def _silu(x):
    return x * jax.nn.sigmoid(x)


def _layer_norm(v, g, b):
    mu = jnp.mean(v, axis=-1, keepdims=True)
    cen = v - mu
    var = jnp.mean(cen * cen, axis=-1, keepdims=True)
    return cen * lax.rsqrt(var + LN_EPS) * g + b


def _mod_kernel(c_ref, w_ref, b_ref, o_ref):
    s = _silu(c_ref[...])
    o_ref[0] = jnp.dot(s, w_ref[0], precision=HIGHEST, preferred_element_type=F32) + b_ref[0]


def _modulation(cvecs, w_mod, b_mod):
    R, D = cvecs.shape
    depth, _, six_d = w_mod.shape
    tn = 1536
    out = pl.pallas_call(
        _mod_kernel,
        grid=(depth, six_d // tn),
        in_specs=[
            pl.BlockSpec((R, D), lambda i, n: (0, 0)),
            pl.BlockSpec((1, D, tn), lambda i, n: (i, 0, n)),
            pl.BlockSpec((1, 1, tn), lambda i, n: (i, 0, n)),
        ],
        out_specs=pl.BlockSpec((1, R, tn), lambda i, n: (i, 0, n)),
        out_shape=jax.ShapeDtypeStruct((depth, R, six_d), F32),
        compiler_params=_params(2),
    )(cvecs, w_mod, b_mod.reshape(depth, 1, six_d))
    return out.reshape(depth, R, 6, D)


def _pool_kernel(x_ref, mod_ref, band_ref, icnt_ref, w_ref, ps_ref, o_ref, *, grid_rows):
    L, Cg = x_ref.shape[1], x_ref.shape[2]
    slab = band_ref.shape[1]
    g = pl.program_id(1)

    def body(k):
        h = x_ref[0] * (1.0 + mod_ref[0, 1:2, :]) + mod_ref[0, 0:1, :]
        if grid_rows is None:
            s = h
        else:
            r3 = h.reshape(grid_rows, L // grid_rows, Cg)
            zpad = jnp.zeros((k // 2,) + r3.shape[1:], F32)
            win = jnp.concatenate([zpad, r3, zpad], axis=0)
            span = 1
            while span < k:
                win = win[:-span] + win[span:]
                span *= 2
            s = win[:grid_rows].reshape(L, Cg)
        band = band_ref[0]
        w = w_ref[0].astype(BF16)
        for i in range(L // slab):
            sl = slice(i * slab, (i + 1) * slab)
            s_i = s[sl]
            hi = s_i.astype(BF16)
            lo = (s_i - hi.astype(F32)).astype(BF16)
            tot = jnp.dot(band, hi, preferred_element_type=F32) + jnp.dot(band, lo, preferred_element_type=F32)
            ic = icnt_ref[0, sl, :]
            mean = tot * jnp.concatenate([ic] * (Cg // V7X_LANES), axis=-1)
            p = mean - h[sl]
            y = jnp.dot(p.astype(BF16), w, preferred_element_type=F32)
            o_ref[0, sl, :] = y * ps_ref[...]

    if grid_rows is None:
        body(None)
    else:
        for gi, k in enumerate(POOL_WINDOWS):
            pl.when(g == gi)(functools.partial(body, k))


def _pool_tables(L, grid_rows, slab):
    n_g = len(POOL_WINDOWS)
    band = np.zeros((n_g, slab, slab), np.float32)
    icnt = np.zeros((n_g, L), np.float32)
    t = np.arange(L)
    for gi, k in enumerate(POOL_WINDOWS):
        if grid_rows is None:
            pos, n, blk = np.arange(slab), L, np.zeros(slab, np.int64)
            cnt = np.clip(t + k - k // 2, 0, L) - np.clip(t - k // 2, 0, L)
        else:
            w = L // grid_rows
            pos, n, blk = np.arange(slab) % w, w, np.arange(slab) // w
            row, col = t // w, t % w
            cnt_r = np.clip(row + k - k // 2, 0, grid_rows) - np.clip(row - k // 2, 0, grid_rows)
            cnt_c = np.clip(col + k - k // 2, 0, w) - np.clip(col - k // 2, 0, w)
            cnt = cnt_r * cnt_c
        lo = np.clip(pos - k // 2, 0, n)
        hi = np.clip(pos + k - k // 2, 0, n)
        inside = (pos[None, :] >= lo[:, None]) & (pos[None, :] < hi[:, None]) & (blk[None, :] == blk[:, None])
        band[gi] = inside.astype(np.float32)
        icnt[gi] = 1.0 / cnt
    icnt = np.broadcast_to(icnt[:, :, None], (n_g, L, V7X_LANES))
    return jnp.asarray(band, BF16), jnp.asarray(icnt, F32)


def _pool_mixer(x, mod, mod_row, pool_w, pool_scale, grid_rows):
    B, L, D = x.shape
    n_g = len(POOL_WINDOWS)
    Cg = D // n_g
    slab = 256
    assert L % slab == 0 and (grid_rows is None or (L == slab or slab % (L // grid_rows) == 0))
    if grid_rows is None:
        assert L == slab
    band, icnt = _pool_tables(L, grid_rows, slab)
    return pl.pallas_call(
        functools.partial(_pool_kernel, grid_rows=grid_rows),
        grid=(B, n_g),
        in_specs=[
            pl.BlockSpec((1, L, Cg), lambda b, g: (b, 0, g)),
            pl.BlockSpec((1, 6, Cg), lambda b, g: (mod_row(b), 0, g)),
            pl.BlockSpec((1, slab, slab), lambda b, g: (g, 0, 0)),
            pl.BlockSpec((1, L, V7X_LANES), lambda b, g: (g, 0, 0)),
            pl.BlockSpec((1, Cg, Cg), lambda b, g: (g, 0, 0)),
            pl.BlockSpec((1, Cg), lambda b, g: (0, g)),
        ],
        out_specs=pl.BlockSpec((1, L, Cg), lambda b, g: (b, 0, g)),
        out_shape=jax.ShapeDtypeStruct((B, L, D), F32),
        compiler_params=_params(2),
    )(x, mod, band, icnt, pool_w, pool_scale.reshape(1, D))


def _route(scores, bias):
    sel = [s + b for s, b in zip(scores, bias)]
    E = EXPERTS_PER_GROUP
    grp = []
    for gi in range(N_GROUPS):
        v = sel[gi * E:(gi + 1) * E]
        best = None
        for a in range(E):
            for c in range(a + 1, E):
                pair = v[a] + v[c]
                best = pair if best is None else jnp.maximum(best, pair)
        grp.append(best)
    g_idx = jnp.zeros_like(grp[0], dtype=jnp.int32)
    g_best = grp[0]
    for gi in range(1, N_GROUPS):
        better = grp[gi] > g_best
        g_idx = jnp.where(better, gi, g_idx)
        g_best = jnp.where(better, grp[gi], g_best)
    in_sel, in_score = [], []
    for l in range(E):
        vs, vc = sel[l], scores[l]
        for gi in range(1, N_GROUPS):
            vs = jnp.where(g_idx == gi, sel[gi * E + l], vs)
            vc = jnp.where(g_idx == gi, scores[gi * E + l], vc)
        in_sel.append(vs)
        in_score.append(vc)
    i1 = jnp.zeros_like(g_idx)
    m1 = in_sel[0]
    for l in range(1, E):
        better = in_sel[l] > m1
        i1 = jnp.where(better, l, i1)
        m1 = jnp.where(better, in_sel[l], m1)
    i2 = jnp.full_like(g_idx, -1)
    m2 = jnp.full_like(m1, -jnp.inf)
    for l in range(E):
        better = (i1 != l) & ((in_sel[l] > m2) | (i2 < 0))
        i2 = jnp.where(better, l, i2)
        m2 = jnp.where(better, in_sel[l], m2)
    s1 = in_score[0]
    s2 = in_score[0]
    for l in range(1, E):
        s1 = jnp.where(i1 == l, in_score[l], s1)
        s2 = jnp.where(i2 == l, in_score[l], s2)
    tot = s1 + s2
    w1, w2 = s1 / tot, s2 / tot
    e1 = g_idx * E + i1
    e2 = g_idx * E + i2
    return [jnp.where(e1 == e, w1, 0.0) + jnp.where(e2 == e, w2, 0.0) for e in range(N_EXPERTS)]


def _post_router_kernel(*refs, has_proj):
    if has_proj:
        x_ref, y_ref, wo_ref, mod_ref, ln_ref, wr_ref, br_ref, x1_ref, h_ref, cmb_ref = refs
        y = jnp.dot(y_ref[...].astype(BF16), wo_ref[...], preferred_element_type=F32)
    else:
        x_ref, y_ref, mod_ref, ln_ref, wr_ref, br_ref, x1_ref, h_ref, cmb_ref = refs
        y = y_ref[...]
    v = DEEPNORM_ALPHA * x_ref[...] + mod_ref[0, 2:3, :] * y
    x1 = _layer_norm(v, ln_ref[0:1, :], ln_ref[1:2, :])
    x1_ref[...] = x1
    h = x1 * (1.0 + mod_ref[0, 4:5, :]) + mod_ref[0, 3:4, :]
    h_ref[...] = h.astype(BF16)
    logits = lax.dot_general(wr_ref[...], h, (((1,), (1,)), ((), ())), precision=HIGHEST,
                             preferred_element_type=F32)
    scores = jax.nn.sigmoid(logits)
    T = scores.shape[1]
    bias = br_ref[...]
    rows = _route([scores[e:e + 1, :] for e in range(N_EXPERTS)],
                  [jnp.broadcast_to(bias[e:e + 1, :], (1, T)) for e in range(N_EXPERTS)])
    for e in range(N_EXPERTS):
        cmb_ref[e:e + 1, :] = rows[e]


def _post_router(x, y, w_out, mod, mod_row, ln, w_router_t, b_router, tile):
    n_tok, D = x.shape
    has_proj = w_out is not None
    tok = pl.BlockSpec((tile, D), lambda t: (t, 0))
    in_specs = [tok, tok]
    args = [x, y]
    if has_proj:
        in_specs.append(_resident(w_out.shape))
        args.append(w_out)
    in_specs += [
        pl.BlockSpec((1, 6, D), lambda t: (mod_row(t), 0, 0)),
        _resident(ln.shape),
        _resident(w_router_t.shape),
        _resident((N_EXPERTS, 1)),
    ]
    args += [mod, ln, w_router_t, b_router.reshape(N_EXPERTS, 1)]
    return pl.pallas_call(
        functools.partial(_post_router_kernel, has_proj=has_proj),
        grid=(n_tok // tile,),
        in_specs=in_specs,
        out_specs=[tok, tok, pl.BlockSpec((N_EXPERTS, tile), lambda t: (0, t))],
        out_shape=[
            jax.ShapeDtypeStruct((n_tok, D), F32),
            jax.ShapeDtypeStruct((n_tok, D), BF16),
            jax.ShapeDtypeStruct((N_EXPERTS, n_tok), F32),
        ],
        compiler_params=_params(1),
    )(*args)


def _moe_kernel(h_ref, cmb_ref, x1_ref, mod_ref, ln_ref, wgu_ref, wd_ref, o_ref, hid_ref):
    F = wgu_ref.shape[2] // 2
    h = h_ref[...]
    cmb = cmb_ref[...]
    for e in range(N_EXPERTS):
        gu = jnp.dot(h, wgu_ref[e], preferred_element_type=F32)
        hid = _silu(gu[:, :F]) * gu[:, F:] * cmb[:, e:e + 1]
        hid_ref[:, e * F:(e + 1) * F] = hid.astype(BF16)
    y = jnp.dot(hid_ref[...], wd_ref[...], preferred_element_type=F32)
    v = DEEPNORM_ALPHA * x1_ref[...] + mod_ref[0, 5:6, :] * y
    o_ref[...] = _layer_norm(v, ln_ref[0:1, :], ln_ref[1:2, :])


def _moe_post(h, cmb, x1, mod, mod_row, ln, w_gu, w_d, tile):
    n_tok, D = x1.shape
    E, _, F2 = w_gu.shape
    tok = pl.BlockSpec((tile, D), lambda t: (t, 0))
    return pl.pallas_call(
        _moe_kernel,
        grid=(n_tok // tile,),
        in_specs=[
            tok,
            pl.BlockSpec((tile, E), lambda t: (t, 0)),
            tok,
            pl.BlockSpec((1, 6, D), lambda t: (mod_row(t), 0, 0)),
            _resident(ln.shape),
            _resident(w_gu.shape),
            _resident(w_d.shape),
        ],
        out_specs=tok,
        out_shape=jax.ShapeDtypeStruct((n_tok, D), F32),
        scratch_shapes=[pltpu.VMEM((tile, E * F2 // 2), BF16)],
        compiler_params=_params(1),
    )(h, cmb, x1, mod, ln, w_gu, w_d)


def _proj_kernel(x_ref, xp_ref, xn_ref, mod_ref, wqkv_ref, wz_ref, wab_ref, cw_ref, alog_ref, dtb_ref,
                 q_ref, k_ref, v_ref, z_ref, grow_ref, gcol_ref, *, tiles_per_seq):
    T = x_ref.shape[0]
    H, dk = q_ref.shape[1], q_ref.shape[3]
    hk = H * dk
    H2 = alog_ref.shape[0]
    C = DN_CHUNK
    halo = xp_ref.shape[0]
    n_taps = DN_CONV_TAPS
    scale = 1.0 + mod_ref[0, 1:2, :]
    shift = mod_ref[0, 0:1, :]
    h = x_ref[...] * scale + shift
    pos = pl.program_id(0) % tiles_per_seq
    prev_ok = (pos != 0).astype(F32)
    next_ok = (pos != tiles_per_seq - 1).astype(F32)
    h_ext = jnp.concatenate([xp_ref[...] * scale + shift, h, xn_ref[...] * scale + shift], axis=0)
    u = jnp.dot(h_ext.astype(BF16), wqkv_ref[...], preferred_element_type=F32)
    rows = lax.broadcasted_iota(jnp.int32, (T + 2 * halo, 1), 0)
    u = u * jnp.where(rows < halo, prev_ok, jnp.where(rows >= T + halo, next_ok, 1.0))
    acc = None
    for j in range(n_taps):
        d = j - n_taps // 2
        shifted = u if d == 0 else pltpu.roll(u, (-d) % (T + 2 * halo), axis=0)
        term = shifted[halo:halo + T] * cw_ref[j:j + 1, :]
        acc = term if acc is None else acc + term
    act = _silu(acc)
    for hd in range(H):
        qh = act[:, hd * dk:(hd + 1) * dk]
        kh = act[:, hk + hd * dk:hk + (hd + 1) * dk]
        q_ref[0, hd] = (qh * (lax.rsqrt(jnp.sum(qh * qh, axis=-1, keepdims=True) + 1e-6) * dk ** -0.5)).astype(BF16)
        k_ref[0, hd] = (kh * lax.rsqrt(jnp.sum(kh * kh, axis=-1, keepdims=True) + 1e-6)).astype(BF16)
        v_ref[0, hd] = act[:, 2 * hk + hd * dk:2 * hk + (hd + 1) * dk].astype(BF16)
    h_hi = h.astype(BF16)
    z = jnp.dot(h_hi, wz_ref[...], preferred_element_type=F32)
    for hd in range(H):
        z_ref[0, hd] = z[:, hd * dk:(hd + 1) * dk].astype(BF16)
    h_lo = (h - h_hi.astype(F32)).astype(BF16)
    nt = (((1,), (1,)), ((), ()))
    ab = (lax.dot_general(wab_ref[0], h_hi, nt, preferred_element_type=F32)
          + lax.dot_general(wab_ref[1], h_hi, nt, preferred_element_type=F32)
          + lax.dot_general(wab_ref[0], h_lo, nt, preferred_element_type=F32))
    a = ab[:H2] + dtb_ref[...]
    softplus = jnp.maximum(a, 0.0) + jnp.log1p(jnp.exp(-jnp.abs(a)))
    g = -jnp.exp(alog_ref[...]) * softplus
    beta = jax.nn.sigmoid(ab[H2:])
    pos = lax.broadcasted_iota(jnp.int32, (H2, T), 1) & (C - 1)
    pre = suf = g
    s = 1
    while s < C:
        pre = pre + jnp.where(pos >= s, pltpu.roll(pre, s, axis=1), 0.0)
        suf = suf + jnp.where(pos < C - s, pltpu.roll(suf, T - s, axis=1), 0.0)
        s *= 2
    tot = pre + suf - g
    fwd_rows = lax.broadcasted_iota(jnp.int32, (H2, T), 0) < H2 // 2
    gc = jnp.where(fwd_rows, pre, suf)
    grow_ref[0] = jnp.concatenate([gc, beta], axis=0)
    n_hg, Hb = gcol_ref.shape[0], H // gcol_ref.shape[0]
    for hg in range(n_hg):
        picks = [t[d * H + hg * Hb:d * H + (hg + 1) * Hb] for t in (gc, beta, tot) for d in range(2)]
        picks.append(jnp.zeros((V7X_LANES - 6 * Hb, T), F32))
        gcol_ref[hg] = jnp.concatenate(picks, axis=0).T


def _gdn_proj(x, B, mod, mod_row, w_qkv, w_z, w_ab, conv_w, a_log, dt_bias, tile):
    n_tok, D = x.shape
    L = n_tok // B
    H = DN_HEADS
    dk = w_z.shape[1] // H
    H2 = a_log.size
    n_hg = H // GDN_HEADS_PER_STEP
    w_ab_t = w_ab.T
    w_ab_hi = w_ab_t.astype(BF16)
    w_ab_split = jnp.stack([w_ab_hi, (w_ab_t - w_ab_hi.astype(F32)).astype(BF16)])
    halo = 8
    tps = L // tile
    n_halo_blocks = n_tok // halo
    head_major = pl.BlockSpec((1, H, tile, dk), lambda t: (t // tps, 0, t % tps, 0))
    qkvz = jax.ShapeDtypeStruct((B, H, L, dk), BF16)
    return pl.pallas_call(
        functools.partial(_proj_kernel, tiles_per_seq=tps),
        grid=(n_tok // tile,),
        in_specs=[
            pl.BlockSpec((tile, D), lambda t: (t, 0)),
            pl.BlockSpec((halo, D), lambda t: (jnp.maximum(t * (tile // halo) - 1, 0), 0)),
            pl.BlockSpec((halo, D), lambda t: (jnp.minimum((t + 1) * (tile // halo), n_halo_blocks - 1), 0)),
            pl.BlockSpec((1, 6, D), lambda t: (mod_row(t), 0, 0)),
            _resident(w_qkv.shape),
            _resident(w_z.shape),
            _resident(w_ab_split.shape),
            _resident(conv_w.shape),
            _resident((H2, 1)),
            _resident((H2, 1)),
        ],
        out_specs=[head_major, head_major, head_major, head_major,
                   pl.BlockSpec((1, 2 * H2, tile), lambda t: (t // tps, 0, t % tps)),
                   pl.BlockSpec((n_hg, tile, V7X_LANES), lambda t: (0, t, 0))],
        out_shape=[qkvz, qkvz, qkvz, qkvz, jax.ShapeDtypeStruct((B, 2 * H2, L), F32),
                   jax.ShapeDtypeStruct((n_hg, n_tok, V7X_LANES), F32)],
        compiler_params=_params(1),
    )(x, x, x, mod, w_qkv, w_z, w_ab_split, conv_w, a_log.reshape(H2, 1), dt_bias.reshape(H2, 1))


def _gdn_kernel(q_ref, k_ref, v_ref, z_ref, gcol_ref, grow_ref, s0f_ref, s0b_ref, ng_ref,
                o_ref, sf_ref, sb_ref,
                s_scr, wq_scr, u_scr, qk_scr, kd_scr, gl_scr, oacc_scr, *, group):
    Hb, L, dk = q_ref.shape[1], q_ref.shape[2], q_ref.shape[3]
    C = DN_CHUNK
    N = L // C
    n_groups = N // group
    gtok = group * C

    m = Hb * group
    for hh in range(Hb):
        s_scr[hh] = s0f_ref[0, hh]
        s_scr[Hb + hh] = s0b_ref[0, hh]
    oacc_scr[...] = jnp.zeros(oacc_scr.shape, F32)

    ii = lax.broadcasted_iota(jnp.int32, (C, C), 0)
    jj = lax.broadcasted_iota(jnp.int32, (C, C), 1)
    eye = (ii == jj).astype(F32)
    off_masks = []
    s = 1
    while s < C:
        same = (ii & -(2 * s)) == (jj & -(2 * s))
        lo = (same & ((ii & s) != 0) & ((jj & s) == 0)).astype(F32)
        up = (same & ((jj & s) != 0) & ((ii & s) == 0)).astype(F32)
        off_masks.append(jnp.stack([lo, up])[:, None])
        s *= 2

    def bmm(a, b):
        return jnp.einsum('nij,njd->nid', a, b, preferred_element_type=F32)

    def bmm_nt(a, b):
        return jnp.einsum('nid,njd->nij', a, b, preferred_element_type=F32)

    def heads(read):
        return jnp.concatenate([read(hh) for hh in range(Hb)], axis=0)

    def local(cgs):
        per_dir = []
        for d in range(2):
            tok = pl.ds(pl.multiple_of(cgs[d] * gtok, gtok), gtok)
            rows = pl.ds(pl.multiple_of(cgs[d] * group, group), group)
            q = heads(lambda hh: q_ref[0, hh, tok, :].reshape(group, C, dk))
            k = heads(lambda hh: k_ref[0, hh, tok, :].reshape(group, C, dk))
            v = heads(lambda hh: v_ref[0, hh, tok, :].reshape(group, C, dk))
            gcols = gcol_ref[0, tok, :]

            def col(kind):
                lane0 = kind * 2 * Hb + d * Hb
                return heads(lambda hh: gcols[:, lane0 + hh:lane0 + hh + 1].reshape(group, C, 1))

            gc, beta, gt = col(0), col(1), col(2)
            gc_row = heads(lambda hh: grow_ref[0, hh, d, rows, :])[:, None, :]
            beta_row = heads(lambda hh: grow_ref[0, hh, 2 + d, rows, :])[:, None, :]
            strict = (ii > jj) if d == 0 else (ii < jj)
            decay = jnp.exp(jnp.where(strict[None], gc - gc_row, -jnp.inf))
            a = bmm_nt(k, k) * beta * decay
            qk_scr[d * Hb:(d + 1) * Hb] = (bmm_nt(q, k) * (decay + eye[None])).astype(BF16).reshape(Hb, group, C, C)
            q_dec = (q.astype(F32) * jnp.exp(gc)).astype(BF16)
            kd_scr[d * Hb:(d + 1) * Hb] = (k.astype(F32) * jnp.exp(gt - gc)).astype(BF16).reshape(Hb, group, C, dk)
            gl_scr[d * Hb:(d + 1) * Hb] = jnp.broadcast_to(jnp.exp(gt[:, 0:1, :]), (m, 8, dk)).reshape(Hb, group, 8, dk)
            per_dir.append((a, k, v, q_dec, gc_row, beta_row))
        a4 = jnp.stack([per_dir[0][0], per_dir[1][0]])
        t_inv = eye[None] - (a4 * off_masks[0]).reshape(2 * m, C, C)
        for mask in off_masks[1:]:
            t_b = t_inv.astype(BF16)
            a_off = (a4 * mask).reshape(2 * m, C, C).astype(BF16)
            t_inv = t_inv - bmm(t_b, bmm(a_off, t_b).astype(BF16))
        for d in range(2):
            _, k, v, q_dec, gc_row, beta_row = per_dir[d]
            t_beta = t_inv[d * m:(d + 1) * m] * beta_row
            u_scr[d * Hb:(d + 1) * Hb] = bmm(t_beta.astype(BF16), v).reshape(Hb, group, C, dk)
            w = bmm((t_beta * jnp.exp(gc_row)).astype(BF16), k)
            wq_scr[d * Hb:(d + 1) * Hb] = jnp.concatenate([w.astype(BF16), q_dec], axis=1).reshape(Hb, group, 2 * C, dk)

    def step(cgs, i):
        slots = [(d, hh) for d in range(2) for hh in range(Hb)]
        ns = [i if d == 0 else group - 1 - i for d, _ in slots]
        states = [s_scr[c] for c in range(2 * Hb)]
        r = [jnp.dot(wq_scr[c, ns[c]], states[c].astype(BF16), preferred_element_type=F32)
             for c in range(2 * Hb)]
        v_new = [(u_scr[c, ns[c]] - r[c][:C]).astype(BF16) for c in range(2 * Hb)]
        o = [jnp.dot(qk_scr[c, ns[c]], v_new[c], preferred_element_type=F32) for c in range(2 * Hb)]
        upd = [lax.dot_general(kd_scr[c, ns[c]], v_new[c], (((0,), (0,)), ((), ())), preferred_element_type=F32)
               for c in range(2 * Hb)]
        for c, (d, hh) in enumerate(slots):
            s_scr[c] = states[c] * gl_scr[c, ns[c]][0:1, :] + upd[c]
            rows = pl.ds(pl.multiple_of((cgs[d] * group + ns[c]) * C, C), C)
            oacc_scr[hh, rows, :] += r[c][C:] + o[c]

    def group_body(gi, _):
        cgs = (gi, n_groups - 1 - gi)
        local(cgs)
        lax.fori_loop(0, group, lambda i, _: (step(cgs, i), 0)[1], 0)
        return 0

    lax.fori_loop(0, n_groups, group_body, 0)

    for hh in range(Hb):
        sf_ref[0, hh] = s_scr[hh]
        sb_ref[0, hh] = s_scr[Hb + hh]
        o = oacc_scr[hh]
        o = o * lax.rsqrt(jnp.mean(o * o, axis=-1, keepdims=True) + 1e-6) * ng_ref[...]
        o_ref[0, :, hh * dk:(hh + 1) * dk] = (o * _silu(z_ref[0, hh].astype(F32))).astype(BF16)


def _gdn_core(q, k, v, z, gcol, grow, s0f, s0b, norm_g):
    B, H, L, dk = q.shape
    C = DN_CHUNK
    N = L // C
    group = min(N, GDN_GROUP_CHUNKS)
    Hb = GDN_HEADS_PER_STEP
    n_chains = 2 * Hb
    seq = pl.BlockSpec((1, Hb, L, dk), lambda b, h: (b, h, 0, 0))
    st = pl.BlockSpec((1, Hb, dk, dk), lambda b, h: (b, h, 0, 0))
    return pl.pallas_call(
        functools.partial(_gdn_kernel, group=group),
        grid=(B, H // Hb),
        in_specs=[
            seq, seq, seq, seq,
            pl.BlockSpec((1, L, V7X_LANES), lambda b, h: (h, b, 0)),
            pl.BlockSpec((1, Hb, 4, N, C), lambda b, h: (b, h, 0, 0, 0)),
            st, st,
            pl.BlockSpec((1, dk), lambda b, h: (0, 0)),
        ],
        out_specs=[pl.BlockSpec((1, L, Hb * dk), lambda b, h: (b, 0, h)), st, st],
        out_shape=[
            jax.ShapeDtypeStruct((B, L, H * dk), BF16),
            jax.ShapeDtypeStruct((B, H, dk, dk), F32),
            jax.ShapeDtypeStruct((B, H, dk, dk), F32),
        ],
        scratch_shapes=[
            pltpu.VMEM((n_chains, dk, dk), F32),
            pltpu.VMEM((n_chains, group, 2 * C, dk), BF16),
            pltpu.VMEM((n_chains, group, C, dk), F32),
            pltpu.VMEM((n_chains, group, C, C), BF16),
            pltpu.VMEM((n_chains, group, C, dk), BF16),
            pltpu.VMEM((n_chains, group, 8, dk), F32),
            pltpu.VMEM((Hb, L, dk), F32),
        ],
        compiler_params=_params(2),
    )(q, k, v, z, gcol, grow, s0f, s0b, norm_g.reshape(1, dk))


def _gdn_mixer(x_tok, B, mod, mod_row, w_qkv, w_z, w_ab, conv_w, a_log, dt_bias, norm_g, s0f, s0b, tile):
    n_tok = x_tok.shape[0]
    L = n_tok // B
    H, C = DN_HEADS, DN_CHUNK
    q, k, v, z, grows, gcol = _gdn_proj(x_tok, B, mod, mod_row, w_qkv, w_z, w_ab, conv_w, a_log, dt_bias, tile)
    grow = grows.reshape(B, 2, 2, H, L).transpose(0, 3, 1, 2, 4).reshape(B, H, 4, L // C, C)
    o, s_f, s_b = _gdn_core(q, k, v, z, gcol, grow, s0f, s0b, norm_g)
    return o.reshape(n_tok, -1), s_f, s_b


def kernel(x, c, ctx, c_ctx, w_mod, b_mod, ln_g, ln_b, pool_w, pool_scale, dn_w_in, dn_conv, dn_a_log, dn_dt_bias, dn_norm, dn_w_out, w_router, b_router, w_gate, w_up, w_down):
    B, L, D = x.shape
    Lc = ctx.shape[1]
    E = w_router.shape[1]
    F = w_gate.shape[-1]
    H = DN_HEADS
    dk = dn_norm.shape[-1]
    hk = H * dk
    ctx_row = B

    n_rows = -(-(B + 1) // 8) * 8
    cvecs = jnp.concatenate([c, c_ctx[None, :], jnp.zeros((n_rows - B - 1, D), F32)], axis=0)
    mods = _modulation(cvecs, w_mod, b_mod)
    ln = jnp.stack([ln_g, ln_b], axis=2)
    w_router_t = w_router.T
    tile = 512
    x_row = lambda t: t // (L // tile)
    c_row = lambda t: ctx_row

    mod = mods[0]
    w_gu = jnp.concatenate([w_gate[0], w_up[0]], axis=-1).astype(BF16)
    w_d = w_down[0].reshape(E * F, D).astype(BF16)
    streams = []
    for tok, rows, row_of_batch, row_of_tile in ((x, L // GRID_W, lambda b: b, x_row), (ctx, None, lambda b: ctx_row, c_row)):
        y = _pool_mixer(tok, mod, row_of_batch, pool_w[0], pool_scale[0], rows)
        tok2 = tok.reshape(-1, D)
        x1, h, cmb_t = _post_router(tok2, y.reshape(-1, D), None, mod, row_of_tile, ln[0, 0], w_router_t, b_router, tile)
        streams.append(_moe_post(h, cmb_t.T, x1, mod, row_of_tile, ln[0, 1], w_gu, w_d, tile))
    x_tok, ctx_tok = streams

    mod = mods[1]
    w_in = dn_w_in[0]
    w_qkv = w_in[:, :3 * hk].astype(BF16)
    w_z = w_in[:, 3 * hk:4 * hk].astype(BF16)
    w_ab = w_in[:, 4 * hk:]
    conv_w = jnp.concatenate([dn_conv[0], jnp.zeros((8 - dn_conv.shape[1], 3 * hk), F32)], axis=0)
    a_log, dt_bias, norm_g = dn_a_log[0], dn_dt_bias[0], dn_norm[0]
    zeros = jnp.zeros((B, H, dk, dk), F32)
    _, s_f, s_b = _gdn_mixer(ctx_tok, B, mod, c_row, w_qkv, w_z, w_ab, conv_w, a_log, dt_bias, norm_g, zeros, zeros,
                             min(tile, Lc))
    o, _, _ = _gdn_mixer(x_tok, B, mod, x_row, w_qkv, w_z, w_ab, conv_w, a_log, dt_bias, norm_g, s_f, s_b, tile)
    w_gu = jnp.concatenate([w_gate[1], w_up[1]], axis=-1).astype(BF16)
    w_d = w_down[1].reshape(E * F, D).astype(BF16)
    x1, h, cmb_t = _post_router(x_tok, o, dn_w_out[0].astype(BF16), mod, x_row, ln[1, 0], w_router_t, b_router, tile)
    out = _moe_post(h, cmb_t.T, x1, mod, x_row, ln[1, 1], w_gu, w_d, tile)
    return out.reshape(B, L, D)
```

```python
import functools

import numpy as np
import jax
import jax.numpy as jnp
from jax import lax
from jax.experimental import pallas as pl
from jax.experimental.pallas import tpu as pltpu

GRID_W = 64
POOL_WINDOWS = (2, 4, 8, 16)
DN_HEADS = 8
DN_CHUNK = 64
DN_CONV_TAPS = 5
N_GROUPS = 4
EXPERTS_PER_GROUP = 4
N_EXPERTS = N_GROUPS * EXPERTS_PER_GROUP
DEPTH = 2
DEEPNORM_ALPHA = (2.0 * DEPTH) ** 0.25
LN_EPS = 1e-5

V7X_LANES = 128
V7X_VMEM_LIMIT_BYTES = 56 * 1024 * 1024

GDN_GROUP_CHUNKS = 8
GDN_HEADS_PER_STEP = 4

F32 = jnp.float32
BF16 = jnp.bfloat16
HIGHEST = lax.Precision.HIGHEST


def _params(n_grid_dims):
    return pltpu.CompilerParams(
        dimension_semantics=("arbitrary",) * n_grid_dims,
        vmem_limit_bytes=V7X_VMEM_LIMIT_BYTES,
    )


def _resident(shape):
    zeros = (0,) * len(shape)
    return pl.BlockSpec(shape, lambda *_: zeros, pipeline_mode=pl.Buffered(1))


def _silu(x):
    return x * jax.nn.sigmoid(x)


def _layer_norm(v, g, b):
    mu = jnp.mean(v, axis=-1, keepdims=True)
    cen = v - mu
    var = jnp.mean(cen * cen, axis=-1, keepdims=True)
    return cen * lax.rsqrt(var + LN_EPS) * g + b


def _mod_kernel(c_ref, w_ref, b_ref, o_ref):
    s = _silu(c_ref[...])
    o_ref[0] = jnp.dot(s, w_ref[0], precision=HIGHEST, preferred_element_type=F32) + b_ref[0]


def _modulation(cvecs, w_mod, b_mod):
    R, D = cvecs.shape
    depth, _, six_d = w_mod.shape
    tn = 1536
    out = pl.pallas_call(
        _mod_kernel,
        grid=(depth, six_d // tn),
        in_specs=[
            pl.BlockSpec((R, D), lambda i, n: (0, 0)),
            pl.BlockSpec((1, D, tn), lambda i, n: (i, 0, n)),
            pl.BlockSpec((1, 1, tn), lambda i, n: (i, 0, n)),
        ],
        out_specs=pl.BlockSpec((1, R, tn), lambda i, n: (i, 0, n)),
        out_shape=jax.ShapeDtypeStruct((depth, R, six_d), F32),
        compiler_params=_params(2),
    )(cvecs, w_mod, b_mod.reshape(depth, 1, six_d))
    return out.reshape(depth, R, 6, D)


def _pool_kernel(x_ref, mod_ref, band_ref, icnt_ref, w_ref, ps_ref, o_ref, *, grid_rows):
    L, Cg = x_ref.shape[1], x_ref.shape[2]
    slab = band_ref.shape[1]
    g = pl.program_id(1)

    def body(k):
        h = x_ref[0] * (1.0 + mod_ref[0, 1:2, :]) + mod_ref[0, 0:1, :]
        if grid_rows is None:
            s = h
        else:
            r3 = h.reshape(grid_rows, L // grid_rows, Cg)
            zpad = jnp.zeros((k // 2,) + r3.shape[1:], F32)
            win = jnp.concatenate([zpad, r3, zpad], axis=0)
            span = 1
            while span < k:
                win = win[:-span] + win[span:]
                span *= 2
            s = win[:grid_rows].reshape(L, Cg)
        band = band_ref[0]
        w = w_ref[0].astype(BF16)
        for i in range(L // slab):
            sl = slice(i * slab, (i + 1) * slab)
            s_i = s[sl]
            hi = s_i.astype(BF16)
            lo = (s_i - hi.astype(F32)).astype(BF16)
            tot = jnp.dot(band, hi, preferred_element_type=F32) + jnp.dot(band, lo, preferred_element_type=F32)
            ic = icnt_ref[0, sl, :]
            mean = tot * jnp.concatenate([ic] * (Cg // V7X_LANES), axis=-1)
            p = mean - h[sl]
            y = jnp.dot(p.astype(BF16), w, preferred_element_type=F32)
            o_ref[0, sl, :] = y * ps_ref[...]

    if grid_rows is None:
        body(None)
    else:
        for gi, k in enumerate(POOL_WINDOWS):
            pl.when(g == gi)(functools.partial(body, k))


def _pool_tables(L, grid_rows, slab):
    n_g = len(POOL_WINDOWS)
    band = np.zeros((n_g, slab, slab), np.float32)
    icnt = np.zeros((n_g, L), np.float32)
    t = np.arange(L)
    for gi, k in enumerate(POOL_WINDOWS):
        if grid_rows is None:
            pos, n, blk = np.arange(slab), L, np.zeros(slab, np.int64)
            cnt = np.clip(t + k - k // 2, 0, L) - np.clip(t - k // 2, 0, L)
        else:
            w = L // grid_rows
            pos, n, blk = np.arange(slab) % w, w, np.arange(slab) // w
            row, col = t // w, t % w
            cnt_r = np.clip(row + k - k // 2, 0, grid_rows) - np.clip(row - k // 2, 0, grid_rows)
            cnt_c = np.clip(col + k - k // 2, 0, w) - np.clip(col - k // 2, 0, w)
            cnt = cnt_r * cnt_c
        lo = np.clip(pos - k // 2, 0, n)
        hi = np.clip(pos + k - k // 2, 0, n)
        inside = (pos[None, :] >= lo[:, None]) & (pos[None, :] < hi[:, None]) & (blk[None, :] == blk[:, None])
        band[gi] = inside.astype(np.float32)
        icnt[gi] = 1.0 / cnt
    icnt = np.broadcast_to(icnt[:, :, None], (n_g, L, V7X_LANES))
    return jnp.asarray(band, BF16), jnp.asarray(icnt, F32)


def _pool_mixer(x, mod, mod_row, pool_w, pool_scale, grid_rows):
    B, L, D = x.shape
    n_g = len(POOL_WINDOWS)
    Cg = D // n_g
    slab = 256
    assert L % slab == 0 and (grid_rows is None or (L == slab or slab % (L // grid_rows) == 0))
    if grid_rows is None:
        assert L == slab
    band, icnt = _pool_tables(L, grid_rows, slab)
    return pl.pallas_call(
        functools.partial(_pool_kernel, grid_rows=grid_rows),
        grid=(B, n_g),
        in_specs=[
            pl.BlockSpec((1, L, Cg), lambda b, g: (b, 0, g)),
            pl.BlockSpec((1, 6, Cg), lambda b, g: (mod_row(b), 0, g)),
            pl.BlockSpec((1, slab, slab), lambda b, g: (g, 0, 0)),
            pl.BlockSpec((1, L, V7X_LANES), lambda b, g: (g, 0, 0)),
            pl.BlockSpec((1, Cg, Cg), lambda b, g: (g, 0, 0)),
            pl.BlockSpec((1, Cg), lambda b, g: (0, g)),
        ],
        out_specs=pl.BlockSpec((1, L, Cg), lambda b, g: (b, 0, g)),
        out_shape=jax.ShapeDtypeStruct((B, L, D), F32),
        compiler_params=_params(2),
    )(x, mod, band, icnt, pool_w, pool_scale.reshape(1, D))


def _route(scores, bias):
    sel = [s + b for s, b in zip(scores, bias)]
    E = EXPERTS_PER_GROUP
    grp = []
    for gi in range(N_GROUPS):
        v = sel[gi * E:(gi + 1) * E]
        best = None
        for a in range(E):
            for c in range(a + 1, E):
                pair = v[a] + v[c]
                best = pair if best is None else jnp.maximum(best, pair)
        grp.append(best)
    g_idx = jnp.zeros_like(grp[0], dtype=jnp.int32)
    g_best = grp[0]
    for gi in range(1, N_GROUPS):
        better = grp[gi] > g_best
        g_idx = jnp.where(better, gi, g_idx)
        g_best = jnp.where(better, grp[gi], g_best)
    in_sel, in_score = [], []
    for l in range(E):
        vs, vc = sel[l], scores[l]
        for gi in range(1, N_GROUPS):
            vs = jnp.where(g_idx == gi, sel[gi * E + l], vs)
            vc = jnp.where(g_idx == gi, scores[gi * E + l], vc)
        in_sel.append(vs)
        in_score.append(vc)
    i1 = jnp.zeros_like(g_idx)
    m1 = in_sel[0]
    for l in range(1, E):
        better = in_sel[l] > m1
        i1 = jnp.where(better, l, i1)
        m1 = jnp.where(better, in_sel[l], m1)
    i2 = jnp.full_like(g_idx, -1)
    m2 = jnp.full_like(m1, -jnp.inf)
    for l in range(E):
        better = (i1 != l) & ((in_sel[l] > m2) | (i2 < 0))
        i2 = jnp.where(better, l, i2)
        m2 = jnp.where(better, in_sel[l], m2)
    s1 = in_score[0]
    s2 = in_score[0]
    for l in range(1, E):
        s1 = jnp.where(i1 == l, in_score[l], s1)
        s2 = jnp.where(i2 == l, in_score[l], s2)
    tot = s1 + s2
    w1, w2 = s1 / tot, s2 / tot
    e1 = g_idx * E + i1
    e2 = g_idx * E + i2
    return [jnp.where(e1 == e, w1, 0.0) + jnp.where(e2 == e, w2, 0.0) for e in range(N_EXPERTS)]


def _sublayers_kernel(*refs, has_proj):
    if has_proj:
        x_ref, y_ref, wo_ref, mod_ref, ln_ref, wr_ref, br_ref, wgu_ref, wd_ref, o_ref, hid_ref = refs
        y = jnp.dot(y_ref[...], wo_ref[...], preferred_element_type=F32)
    else:
        x_ref, y_ref, mod_ref, ln_ref, wr_ref, br_ref, wgu_ref, wd_ref, o_ref, hid_ref = refs
        y = y_ref[...]
    T = x_ref.shape[0]
    F = wgu_ref.shape[2] // 2
    v = DEEPNORM_ALPHA * x_ref[...] + mod_ref[0, 2:3, :] * y
    x1 = _layer_norm(v, ln_ref[0:1, :], ln_ref[1:2, :])
    h = x1 * (1.0 + mod_ref[0, 4:5, :]) + mod_ref[0, 3:4, :]
    logits = lax.dot_general(wr_ref[...], h, (((1,), (1,)), ((), ())), precision=HIGHEST,
                             preferred_element_type=F32)
    scores = jax.nn.sigmoid(logits)
    bias = br_ref[...]
    rows = _route([scores[e:e + 1, :] for e in range(N_EXPERTS)],
                  [jnp.broadcast_to(bias[e:e + 1, :], (1, T)) for e in range(N_EXPERTS)])
    cmb = jnp.concatenate(rows + [jnp.zeros((V7X_LANES - N_EXPERTS, T), F32)], axis=0).T
    hb = h.astype(BF16)
    for e in range(N_EXPERTS):
        gu = jnp.dot(hb, wgu_ref[e], preferred_element_type=F32)
        hid = _silu(gu[:, :F]) * gu[:, F:] * cmb[:, e:e + 1]
        hid_ref[:, e * F:(e + 1) * F] = hid.astype(BF16)
    y2 = jnp.dot(hid_ref[...], wd_ref[...], preferred_element_type=F32)
    v2 = DEEPNORM_ALPHA * x1 + mod_ref[0, 5:6, :] * y2
    o_ref[...] = _layer_norm(v2, ln_ref[2:3, :], ln_ref[3:4, :])


def _sublayers(x, y, w_out, mod, mod_row, ln, w_router_t, b_router, w_gu, w_d, tile):
    n_tok, D = x.shape
    E, _, F2 = w_gu.shape
    has_proj = w_out is not None
    tok = pl.BlockSpec((tile, D), lambda t: (t, 0))
    in_specs = [tok, tok]
    args = [x, y]
    if has_proj:
        in_specs.append(_resident(w_out.shape))
        args.append(w_out)
    in_specs += [
        pl.BlockSpec((1, 6, D), lambda t: (mod_row(t), 0, 0)),
        _resident(ln.shape),
        _resident(w_router_t.shape),
        _resident((E, 1)),
        _resident(w_gu.shape),
        _resident(w_d.shape),
    ]
    args += [mod, ln, w_router_t, b_router.reshape(E, 1), w_gu, w_d]
    return pl.pallas_call(
        functools.partial(_sublayers_kernel, has_proj=has_proj),
        grid=(n_tok // tile,),
        in_specs=in_specs,
        out_specs=tok,
        out_shape=jax.ShapeDtypeStruct((n_tok, D), F32),
        scratch_shapes=[pltpu.VMEM((tile, E * F2 // 2), BF16)],
        compiler_params=_params(1),
    )(*args)


def _proj_kernel(x_ref, xp_ref, xn_ref, mod_ref, wqkv_ref, wz_ref, wab_ref, cw_ref, alog_ref, dtb_ref,
                 q_ref, k_ref, v_ref, z_ref, grow_ref, gcol_ref, *, tiles_per_seq):
    T = x_ref.shape[0]
    H, dk = q_ref.shape[1], q_ref.shape[3]
    hk = H * dk
    H2 = alog_ref.shape[0]
    C = DN_CHUNK
    halo = xp_ref.shape[0]
    n_taps = DN_CONV_TAPS
    scale = 1.0 + mod_ref[0, 1:2, :]
    shift = mod_ref[0, 0:1, :]
    h = x_ref[...] * scale + shift
    pos = pl.program_id(0) % tiles_per_seq
    prev_ok = (pos != 0).astype(F32)
    next_ok = (pos != tiles_per_seq - 1).astype(F32)
    h_ext = jnp.concatenate([xp_ref[...] * scale + shift, h, xn_ref[...] * scale + shift], axis=0)
    u = jnp.dot(h_ext.astype(BF16), wqkv_ref[...], preferred_element_type=F32)
    rows = lax.broadcasted_iota(jnp.int32, (T + 2 * halo, 1), 0)
    u = u * jnp.where(rows < halo, prev_ok, jnp.where(rows >= T + halo, next_ok, 1.0))
    acc = None
    for j in range(n_taps):
        d = j - n_taps // 2
        shifted = u if d == 0 else pltpu.roll(u, (-d) % (T + 2 * halo), axis=0)
        term = shifted[halo:halo + T] * cw_ref[j:j + 1, :]
        acc = term if acc is None else acc + term
    act = _silu(acc)
    for hd in range(H):
        qh = act[:, hd * dk:(hd + 1) * dk]
        kh = act[:, hk + hd * dk:hk + (hd + 1) * dk]
        q_ref[0, hd] = (qh * (lax.rsqrt(jnp.sum(qh * qh, axis=-1, keepdims=True) + 1e-6) * dk ** -0.5)).astype(BF16)
        k_ref[0, hd] = (kh * lax.rsqrt(jnp.sum(kh * kh, axis=-1, keepdims=True) + 1e-6)).astype(BF16)
        v_ref[0, hd] = act[:, 2 * hk + hd * dk:2 * hk + (hd + 1) * dk].astype(BF16)
    h_hi = h.astype(BF16)
    z = jnp.dot(h_hi, wz_ref[...], preferred_element_type=F32)
    for hd in range(H):
        z_ref[0, hd] = z[:, hd * dk:(hd + 1) * dk].astype(BF16)
    h_lo = (h - h_hi.astype(F32)).astype(BF16)
    nt = (((1,), (1,)), ((), ()))
    ab = (lax.dot_general(wab_ref[0], h_hi, nt, preferred_element_type=F32)
          + lax.dot_general(wab_ref[1], h_hi, nt, preferred_element_type=F32)
          + lax.dot_general(wab_ref[0], h_lo, nt, preferred_element_type=F32))
    a = ab[:H2] + dtb_ref[...]
    softplus = jnp.maximum(a, 0.0) + jnp.log1p(jnp.exp(-jnp.abs(a)))
    g = -jnp.exp(alog_ref[...]) * softplus
    beta = jax.nn.sigmoid(ab[H2:])
    pos = lax.broadcasted_iota(jnp.int32, (H2, T), 1) & (C - 1)
    pre = suf = g
    s = 1
    while s < C:
        pre = pre + jnp.where(pos >= s, pltpu.roll(pre, s, axis=1), 0.0)
        suf = suf + jnp.where(pos < C - s, pltpu.roll(suf, T - s, axis=1), 0.0)
        s *= 2
    tot = pre + suf - g
    fwd_rows = lax.broadcasted_iota(jnp.int32, (H2, T), 0) < H2 // 2
    gc = jnp.where(fwd_rows, pre, suf)
    grow_ref[0] = jnp.concatenate([gc, beta], axis=0)
    n_hg, Hb = gcol_ref.shape[0], H // gcol_ref.shape[0]
    for hg in range(n_hg):
        picks = [t[d * H + hg * Hb:d * H + (hg + 1) * Hb] for t in (gc, beta, tot) for d in range(2)]
        picks.append(jnp.zeros((V7X_LANES - 6 * Hb, T), F32))
        gcol_ref[hg] = jnp.concatenate(picks, axis=0).T


def _gdn_proj(x, B, mod, mod_row, w_qkv, w_z, w_ab, conv_w, a_log, dt_bias, tile):
    n_tok, D = x.shape
    L = n_tok // B
    H = DN_HEADS
    dk = w_z.shape[1] // H
    H2 = a_log.size
    n_hg = H // GDN_HEADS_PER_STEP
    w_ab_t = w_ab.T
    w_ab_hi = w_ab_t.astype(BF16)
    w_ab_split = jnp.stack([w_ab_hi, (w_ab_t - w_ab_hi.astype(F32)).astype(BF16)])
    halo = 8
    tps = L // tile
    n_halo_blocks = n_tok // halo
    head_major = pl.BlockSpec((1, H, tile, dk), lambda t: (t // tps, 0, t % tps, 0))
    qkvz = jax.ShapeDtypeStruct((B, H, L, dk), BF16)
    return pl.pallas_call(
        functools.partial(_proj_kernel, tiles_per_seq=tps),
        grid=(n_tok // tile,),
        in_specs=[
            pl.BlockSpec((tile, D), lambda t: (t, 0)),
            pl.BlockSpec((halo, D), lambda t: (jnp.maximum(t * (tile // halo) - 1, 0), 0)),
            pl.BlockSpec((halo, D), lambda t: (jnp.minimum((t + 1) * (tile // halo), n_halo_blocks - 1), 0)),
            pl.BlockSpec((1, 6, D), lambda t: (mod_row(t), 0, 0)),
            _resident(w_qkv.shape),
            _resident(w_z.shape),
            _resident(w_ab_split.shape),
            _resident(conv_w.shape),
            _resident((H2, 1)),
            _resident((H2, 1)),
        ],
        out_specs=[head_major, head_major, head_major, head_major,
                   pl.BlockSpec((1, 2 * H2, tile), lambda t: (t // tps, 0, t % tps)),
                   pl.BlockSpec((n_hg, tile, V7X_LANES), lambda t: (0, t, 0))],
        out_shape=[qkvz, qkvz, qkvz, qkvz, jax.ShapeDtypeStruct((B, 2 * H2, L), F32),
                   jax.ShapeDtypeStruct((n_hg, n_tok, V7X_LANES), F32)],
        compiler_params=_params(1),
    )(x, x, x, mod, w_qkv, w_z, w_ab_split, conv_w, a_log.reshape(H2, 1), dt_bias.reshape(H2, 1))


def _gdn_kernel(q_ref, k_ref, v_ref, z_ref, gcol_ref, grow_ref, s0f_ref, s0b_ref, ng_ref,
                o_ref, sf_ref, sb_ref,
                s_scr, wq_scr, u_scr, qk_scr, kd_scr, gl_scr, oacc_scr, *, group):
    Hb, L, dk = q_ref.shape[1], q_ref.shape[2], q_ref.shape[3]
    C = DN_CHUNK
    N = L // C
    n_groups = N // group
    gtok = group * C

    m = Hb * group
    for hh in range(Hb):
        s_scr[hh] = s0f_ref[0, hh]
        s_scr[Hb + hh] = s0b_ref[0, hh]
    oacc_scr[...] = jnp.zeros(oacc_scr.shape, F32)

    ii = lax.broadcasted_iota(jnp.int32, (C, C), 0)
    jj = lax.broadcasted_iota(jnp.int32, (C, C), 1)
    eye = (ii == jj).astype(F32)
    off_masks = []
    s = 1
    while s < C:
        same = (ii & -(2 * s)) == (jj & -(2 * s))
        lo = (same & ((ii & s) != 0) & ((jj & s) == 0)).astype(F32)
        up = (same & ((jj & s) != 0) & ((ii & s) == 0)).astype(F32)
        off_masks.append(jnp.stack([lo, up])[:, None])
        s *= 2

    def bmm(a, b):
        return jnp.einsum('nij,njd->nid', a, b, preferred_element_type=F32)

    def bmm_nt(a, b):
        return jnp.einsum('nid,njd->nij', a, b, preferred_element_type=F32)

    def heads(read):
        return jnp.concatenate([read(hh) for hh in range(Hb)], axis=0)

    def local(cgs):
        per_dir = []
        for d in range(2):
            tok = pl.ds(pl.multiple_of(cgs[d] * gtok, gtok), gtok)
            rows = pl.ds(pl.multiple_of(cgs[d] * group, group), group)
            q = heads(lambda hh: q_ref[0, hh, tok, :].reshape(group, C, dk))
            k = heads(lambda hh: k_ref[0, hh, tok, :].reshape(group, C, dk))
            v = heads(lambda hh: v_ref[0, hh, tok, :].reshape(group, C, dk))
            gcols = gcol_ref[0, tok, :]

            def col(kind):
                lane0 = kind * 2 * Hb + d * Hb
                return heads(lambda hh: gcols[:, lane0 + hh:lane0 + hh + 1].reshape(group, C, 1))

            gc, beta, gt = col(0), col(1), col(2)
            gc_row = heads(lambda hh: grow_ref[0, hh, d, rows, :])[:, None, :]
            beta_row = heads(lambda hh: grow_ref[0, hh, 2 + d, rows, :])[:, None, :]
            strict = (ii > jj) if d == 0 else (ii < jj)
            decay = jnp.exp(jnp.where(strict[None], gc - gc_row, -jnp.inf))
            a = bmm_nt(k, k) * beta * decay
            qk_scr[d * Hb:(d + 1) * Hb] = (bmm_nt(q, k) * (decay + eye[None])).astype(BF16).reshape(Hb, group, C, C)
            q_dec = (q.astype(F32) * jnp.exp(gc)).astype(BF16)
            kd_scr[d * Hb:(d + 1) * Hb] = (k.astype(F32) * jnp.exp(gt - gc)).astype(BF16).reshape(Hb, group, C, dk)
            gl_scr[d * Hb:(d + 1) * Hb] = jnp.broadcast_to(jnp.exp(gt[:, 0:1, :]), (m, 8, dk)).reshape(Hb, group, 8, dk)
            per_dir.append((a, k, v, q_dec, gc_row, beta_row))
        a4 = jnp.stack([per_dir[0][0], per_dir[1][0]])
        t_inv = eye[None] - (a4 * off_masks[0]).reshape(2 * m, C, C)
        for mask in off_masks[1:]:
            t_b = t_inv.astype(BF16)
            a_off = (a4 * mask).reshape(2 * m, C, C).astype(BF16)
            t_inv = t_inv - bmm(t_b, bmm(a_off, t_b).astype(BF16))
        for d in range(2):
            _, k, v, q_dec, gc_row, beta_row = per_dir[d]
            t_beta = t_inv[d * m:(d + 1) * m] * beta_row
            u_scr[d * Hb:(d + 1) * Hb] = bmm(t_beta.astype(BF16), v).reshape(Hb, group, C, dk)
            w = bmm((t_beta * jnp.exp(gc_row)).astype(BF16), k)
            wq_scr[d * Hb:(d + 1) * Hb] = jnp.concatenate([w.astype(BF16), q_dec], axis=1).reshape(Hb, group, 2 * C, dk)

    def step(cgs, i):
        slots = [(d, hh) for d in range(2) for hh in range(Hb)]
        ns = [i if d == 0 else group - 1 - i for d, _ in slots]
        states = [s_scr[c] for c in range(2 * Hb)]
        r = [jnp.dot(wq_scr[c, ns[c]], states[c].astype(BF16), preferred_element_type=F32)
             for c in range(2 * Hb)]
        v_new = [(u_scr[c, ns[c]] - r[c][:C]).astype(BF16) for c in range(2 * Hb)]
        o = [jnp.dot(qk_scr[c, ns[c]], v_new[c], preferred_element_type=F32) for c in range(2 * Hb)]
        upd = [lax.dot_general(kd_scr[c, ns[c]], v_new[c], (((0,), (0,)), ((), ())), preferred_element_type=F32)
               for c in range(2 * Hb)]
        for c, (d, hh) in enumerate(slots):
            s_scr[c] = states[c] * gl_scr[c, ns[c]][0:1, :] + upd[c]
            rows = pl.ds(pl.multiple_of((cgs[d] * group + ns[c]) * C, C), C)
            oacc_scr[hh, rows, :] += r[c][C:] + o[c]

    def group_body(gi, _):
        cgs = (gi, n_groups - 1 - gi)
        local(cgs)
        lax.fori_loop(0, group, lambda i, _: (step(cgs, i), 0)[1], 0)
        return 0

    lax.fori_loop(0, n_groups, group_body, 0)

    for hh in range(Hb):
        sf_ref[0, hh] = s_scr[hh]
        sb_ref[0, hh] = s_scr[Hb + hh]
        o = oacc_scr[hh]
        o = o * lax.rsqrt(jnp.mean(o * o, axis=-1, keepdims=True) + 1e-6) * ng_ref[...]
        o_ref[0, :, hh * dk:(hh + 1) * dk] = (o * _silu(z_ref[0, hh].astype(F32))).astype(BF16)


def _gdn_core(q, k, v, z, gcol, grow, s0f, s0b, norm_g):
    B, H, L, dk = q.shape
    C = DN_CHUNK
    N = L // C
    group = min(N, GDN_GROUP_CHUNKS)
    Hb = GDN_HEADS_PER_STEP
    n_chains = 2 * Hb
    seq = pl.BlockSpec((1, Hb, L, dk), lambda b, h: (b, h, 0, 0))
    st = pl.BlockSpec((1, Hb, dk, dk), lambda b, h: (b, h, 0, 0))
    return pl.pallas_call(
        functools.partial(_gdn_kernel, group=group),
        grid=(B, H // Hb),
        in_specs=[
            seq, seq, seq, seq,
            pl.BlockSpec((1, L, V7X_LANES), lambda b, h: (h, b, 0)),
            pl.BlockSpec((1, Hb, 4, N, C), lambda b, h: (b, h, 0, 0, 0)),
            st, st,
            pl.BlockSpec((1, dk), lambda b, h: (0, 0)),
        ],
        out_specs=[pl.BlockSpec((1, L, Hb * dk), lambda b, h: (b, 0, h)), st, st],
        out_shape=[
            jax.ShapeDtypeStruct((B, L, H * dk), BF16),
            jax.ShapeDtypeStruct((B, H, dk, dk), F32),
            jax.ShapeDtypeStruct((B, H, dk, dk), F32),
        ],
        scratch_shapes=[
            pltpu.VMEM((n_chains, dk, dk), F32),
            pltpu.VMEM((n_chains, group, 2 * C, dk), BF16),
            pltpu.VMEM((n_chains, group, C, dk), F32),
            pltpu.VMEM((n_chains, group, C, C), BF16),
            pltpu.VMEM((n_chains, group, C, dk), BF16),
            pltpu.VMEM((n_chains, group, 8, dk), F32),
            pltpu.VMEM((Hb, L, dk), F32),
        ],
        compiler_params=_params(2),
    )(q, k, v, z, gcol, grow, s0f, s0b, norm_g.reshape(1, dk))


def _gdn_mixer(x_tok, B, mod, mod_row, w_qkv, w_z, w_ab, conv_w, a_log, dt_bias, norm_g, s0f, s0b, tile):
    n_tok = x_tok.shape[0]
    L = n_tok // B
    H, C = DN_HEADS, DN_CHUNK
    q, k, v, z, grows, gcol = _gdn_proj(x_tok, B, mod, mod_row, w_qkv, w_z, w_ab, conv_w, a_log, dt_bias, tile)
    grow = grows.reshape(B, 2, 2, H, L).transpose(0, 3, 1, 2, 4).reshape(B, H, 4, L // C, C)
    o, s_f, s_b = _gdn_core(q, k, v, z, gcol, grow, s0f, s0b, norm_g)
    return o.reshape(n_tok, -1), s_f, s_b


def kernel(x, c, ctx, c_ctx, w_mod, b_mod, ln_g, ln_b, pool_w, pool_scale, dn_w_in, dn_conv, dn_a_log, dn_dt_bias, dn_norm, dn_w_out, w_router, b_router, w_gate, w_up, w_down):
    B, L, D = x.shape
    Lc = ctx.shape[1]
    E = w_router.shape[1]
    F = w_gate.shape[-1]
    H = DN_HEADS
    dk = dn_norm.shape[-1]
    hk = H * dk
    ctx_row = B

    n_rows = -(-(B + 1) // 8) * 8
    cvecs = jnp.concatenate([c, c_ctx[None, :], jnp.zeros((n_rows - B - 1, D), F32)], axis=0)
    mods = _modulation(cvecs, w_mod, b_mod)
    ln = jnp.stack([ln_g, ln_b], axis=2).reshape(ln_g.shape[0], 4, D)
    w_router_t = w_router.T
    tile = 512
    x_row = lambda t: t // (L // tile)
    c_row = lambda t: ctx_row

    mod = mods[0]
    w_gu = jnp.concatenate([w_gate[0], w_up[0]], axis=-1).astype(BF16)
    w_d = w_down[0].reshape(E * F, D).astype(BF16)
    streams = []
    for tok, rows, row_of_batch, row_of_tile in ((x, L // GRID_W, lambda b: b, x_row), (ctx, None, lambda b: ctx_row, c_row)):
        y = _pool_mixer(tok, mod, row_of_batch, pool_w[0], pool_scale[0], rows)
        streams.append(_sublayers(tok.reshape(-1, D), y.reshape(-1, D), None, mod, row_of_tile, ln[0], w_router_t,
                                  b_router, w_gu, w_d, tile))
    x_tok, ctx_tok = streams

    mod = mods[1]
    w_in = dn_w_in[0]
    w_qkv = w_in[:, :3 * hk].astype(BF16)
    w_z = w_in[:, 3 * hk:4 * hk].astype(BF16)
    w_ab = w_in[:, 4 * hk:]
    conv_w = jnp.concatenate([dn_conv[0], jnp.zeros((8 - dn_conv.shape[1], 3 * hk), F32)], axis=0)
    a_log, dt_bias, norm_g = dn_a_log[0], dn_dt_bias[0], dn_norm[0]
    zeros = jnp.zeros((B, H, dk, dk), F32)
    _, s_f, s_b = _gdn_mixer(ctx_tok, B, mod, c_row, w_qkv, w_z, w_ab, conv_w, a_log, dt_bias, norm_g, zeros, zeros,
                             min(tile, Lc))
    o, _, _ = _gdn_mixer(x_tok, B, mod, x_row, w_qkv, w_z, w_ab, conv_w, a_log, dt_bias, norm_g, s_f, s_b, tile)
    w_gu = jnp.concatenate([w_gate[1], w_up[1]], axis=-1).astype(BF16)
    w_d = w_down[1].reshape(E * F, D).astype(BF16)
    out = _sublayers(x_tok, o, dn_w_out[0].astype(BF16), mod, x_row, ln[1], w_router_t, b_router, w_gu, w_d, tile)
    return out.reshape(B, L, D)
```

```python
import functools

import numpy as np
import jax
import jax.numpy as jnp
from jax import lax
from jax.experimental import pallas as pl
from jax.experimental.pallas import tpu as pltpu

GRID_W = 64
POOL_WINDOWS = (2, 4, 8, 16)
DN_HEADS = 8
DN_CHUNK = 64
DN_CONV_TAPS = 5
N_GROUPS = 4
EXPERTS_PER_GROUP = 4
N_EXPERTS = N_GROUPS * EXPERTS_PER_GROUP
DEPTH = 2
DEEPNORM_ALPHA = (2.0 * DEPTH) ** 0.25
LN_EPS = 1e-5

V7X_LANES = 128
V7X_VMEM_LIMIT_BYTES = 56 * 1024 * 1024

GDN_GROUP_CHUNKS = 8


def _gdn_heads_per_step(seq_len):
    return DN_HEADS if seq_len <= GDN_GROUP_CHUNKS * DN_CHUNK else DN_HEADS // 2

F32 = jnp.float32
BF16 = jnp.bfloat16
HIGHEST = lax.Precision.HIGHEST


def _params(n_grid_dims):
    return pltpu.CompilerParams(
        dimension_semantics=("arbitrary",) * n_grid_dims,
        vmem_limit_bytes=V7X_VMEM_LIMIT_BYTES,
    )


def _resident(shape):
    zeros = (0,) * len(shape)
    return pl.BlockSpec(shape, lambda *_: zeros, pipeline_mode=pl.Buffered(1))


def _silu(x):
    return x * jax.nn.sigmoid(x)


def _layer_norm(v, g, b):
    mu = jnp.mean(v, axis=-1, keepdims=True)
    cen = v - mu
    var = jnp.mean(cen * cen, axis=-1, keepdims=True)
    return cen * lax.rsqrt(var + LN_EPS) * g + b


def _mod_kernel(c_ref, w_ref, b_ref, o_ref):
    s = _silu(c_ref[...])
    o_ref[0] = jnp.dot(s, w_ref[0], precision=HIGHEST, preferred_element_type=F32) + b_ref[0]


def _modulation(cvecs, w_mod, b_mod):
    R, D = cvecs.shape
    depth, _, six_d = w_mod.shape
    tn = 1536
    out = pl.pallas_call(
        _mod_kernel,
        grid=(depth, six_d // tn),
        in_specs=[
            pl.BlockSpec((R, D), lambda i, n: (0, 0)),
            pl.BlockSpec((1, D, tn), lambda i, n: (i, 0, n)),
            pl.BlockSpec((1, 1, tn), lambda i, n: (i, 0, n)),
        ],
        out_specs=pl.BlockSpec((1, R, tn), lambda i, n: (i, 0, n)),
        out_shape=jax.ShapeDtypeStruct((depth, R, six_d), F32),
        compiler_params=_params(2),
    )(cvecs, w_mod, b_mod.reshape(depth, 1, six_d))
    return out.reshape(depth, R, 6, D)


def _pool_kernel(x_ref, mod_ref, band_ref, icnt_ref, w_ref, ps_ref, o_ref, *, grid_rows):
    L, Cg = x_ref.shape[1], x_ref.shape[2]
    slab = band_ref.shape[1]
    g = pl.program_id(1)

    def body(k):
        h = x_ref[0] * (1.0 + mod_ref[0, 1:2, :]) + mod_ref[0, 0:1, :]
        if grid_rows is None:
            s = h
        else:
            r3 = h.reshape(grid_rows, L // grid_rows, Cg)
            zpad = jnp.zeros((k // 2,) + r3.shape[1:], F32)
            win = jnp.concatenate([zpad, r3, zpad], axis=0)
            span = 1
            while span < k:
                win = win[:-span] + win[span:]
                span *= 2
            s = win[:grid_rows].reshape(L, Cg)
        band = band_ref[0]
        w = w_ref[0].astype(BF16)
        for i in range(L // slab):
            sl = slice(i * slab, (i + 1) * slab)
            s_i = s[sl]
            hi = s_i.astype(BF16)
            lo = (s_i - hi.astype(F32)).astype(BF16)
            tot = jnp.dot(band, hi, preferred_element_type=F32) + jnp.dot(band, lo, preferred_element_type=F32)
            ic = icnt_ref[0, sl, :]
            mean = tot * jnp.concatenate([ic] * (Cg // V7X_LANES), axis=-1)
            p = mean - h[sl]
            y = jnp.dot(p.astype(BF16), w, preferred_element_type=F32)
            o_ref[0, sl, :] = y * ps_ref[...]

    if grid_rows is None:
        body(None)
    else:
        for gi, k in enumerate(POOL_WINDOWS):
            pl.when(g == gi)(functools.partial(body, k))


def _pool_tables(L, grid_rows, slab):
    n_g = len(POOL_WINDOWS)
    band = np.zeros((n_g, slab, slab), np.float32)
    icnt = np.zeros((n_g, L), np.float32)
    t = np.arange(L)
    for gi, k in enumerate(POOL_WINDOWS):
        if grid_rows is None:
            pos, n, blk = np.arange(slab), L, np.zeros(slab, np.int64)
            cnt = np.clip(t + k - k // 2, 0, L) - np.clip(t - k // 2, 0, L)
        else:
            w = L // grid_rows
            pos, n, blk = np.arange(slab) % w, w, np.arange(slab) // w
            row, col = t // w, t % w
            cnt_r = np.clip(row + k - k // 2, 0, grid_rows) - np.clip(row - k // 2, 0, grid_rows)
            cnt_c = np.clip(col + k - k // 2, 0, w) - np.clip(col - k // 2, 0, w)
            cnt = cnt_r * cnt_c
        lo = np.clip(pos - k // 2, 0, n)
        hi = np.clip(pos + k - k // 2, 0, n)
        inside = (pos[None, :] >= lo[:, None]) & (pos[None, :] < hi[:, None]) & (blk[None, :] == blk[:, None])
        band[gi] = inside.astype(np.float32)
        icnt[gi] = 1.0 / cnt
    icnt = np.broadcast_to(icnt[:, :, None], (n_g, L, V7X_LANES))
    return jnp.asarray(band, BF16), jnp.asarray(icnt, F32)


def _pool_mixer(x, mod, mod_row, pool_w, pool_scale, grid_rows):
    B, L, D = x.shape
    n_g = len(POOL_WINDOWS)
    Cg = D // n_g
    slab = 256
    assert L % slab == 0 and (grid_rows is None or (L == slab or slab % (L // grid_rows) == 0))
    if grid_rows is None:
        assert L == slab
    band, icnt = _pool_tables(L, grid_rows, slab)
    return pl.pallas_call(
        functools.partial(_pool_kernel, grid_rows=grid_rows),
        grid=(B, n_g),
        in_specs=[
            pl.BlockSpec((1, L, Cg), lambda b, g: (b, 0, g)),
            pl.BlockSpec((1, 6, Cg), lambda b, g: (mod_row(b), 0, g)),
            pl.BlockSpec((1, slab, slab), lambda b, g: (g, 0, 0)),
            pl.BlockSpec((1, L, V7X_LANES), lambda b, g: (g, 0, 0)),
            pl.BlockSpec((1, Cg, Cg), lambda b, g: (g, 0, 0)),
            pl.BlockSpec((1, Cg), lambda b, g: (0, g)),
        ],
        out_specs=pl.BlockSpec((1, L, Cg), lambda b, g: (b, 0, g)),
        out_shape=jax.ShapeDtypeStruct((B, L, D), F32),
        compiler_params=_params(2),
    )(x, mod, band, icnt, pool_w, pool_scale.reshape(1, D))


def _route(scores, bias):
    sel = [s + b for s, b in zip(scores, bias)]
    E = EXPERTS_PER_GROUP
    grp = []
    for gi in range(N_GROUPS):
        v = sel[gi * E:(gi + 1) * E]
        best = None
        for a in range(E):
            for c in range(a + 1, E):
                pair = v[a] + v[c]
                best = pair if best is None else jnp.maximum(best, pair)
        grp.append(best)
    g_idx = jnp.zeros_like(grp[0], dtype=jnp.int32)
    g_best = grp[0]
    for gi in range(1, N_GROUPS):
        better = grp[gi] > g_best
        g_idx = jnp.where(better, gi, g_idx)
        g_best = jnp.where(better, grp[gi], g_best)
    in_sel, in_score = [], []
    for l in range(E):
        vs, vc = sel[l], scores[l]
        for gi in range(1, N_GROUPS):
            vs = jnp.where(g_idx == gi, sel[gi * E + l], vs)
            vc = jnp.where(g_idx == gi, scores[gi * E + l], vc)
        in_sel.append(vs)
        in_score.append(vc)
    i1 = jnp.zeros_like(g_idx)
    m1 = in_sel[0]
    for l in range(1, E):
        better = in_sel[l] > m1
        i1 = jnp.where(better, l, i1)
        m1 = jnp.where(better, in_sel[l], m1)
    i2 = jnp.full_like(g_idx, -1)
    m2 = jnp.full_like(m1, -jnp.inf)
    for l in range(E):
        better = (i1 != l) & ((in_sel[l] > m2) | (i2 < 0))
        i2 = jnp.where(better, l, i2)
        m2 = jnp.where(better, in_sel[l], m2)
    s1 = in_score[0]
    s2 = in_score[0]
    for l in range(1, E):
        s1 = jnp.where(i1 == l, in_score[l], s1)
        s2 = jnp.where(i2 == l, in_score[l], s2)
    tot = s1 + s2
    w1, w2 = s1 / tot, s2 / tot
    e1 = g_idx * E + i1
    e2 = g_idx * E + i2
    return [jnp.where(e1 == e, w1, 0.0) + jnp.where(e2 == e, w2, 0.0) for e in range(N_EXPERTS)]


def _sublayers_kernel(*refs, has_proj):
    if has_proj:
        x_ref, y_ref, wo_ref, mod_ref, ln_ref, wr_ref, br_ref, wgu_ref, wd_ref, o_ref, hid_ref, x1_ref, hb_ref = refs
    else:
        x_ref, y_ref, mod_ref, ln_ref, wr_ref, br_ref, wgu_ref, wd_ref, o_ref, hid_ref, x1_ref, hb_ref = refs
    T = x_ref.shape[0]
    F = wgu_ref.shape[2] // 2
    halves = (slice(0, T // 2), slice(T // 2, T))
    nt = (((1,), (1,)), ((), ()))
    n_early = 2
    logits, gus = [], [[] for _ in range(n_early)]
    y_full = jnp.dot(y_ref[...], wo_ref[...], preferred_element_type=F32) if has_proj else None
    for rs in halves:
        y = y_full[rs] if has_proj else y_ref[rs, :]
        v = DEEPNORM_ALPHA * x_ref[rs, :] + mod_ref[0, 2:3, :] * y
        x1 = _layer_norm(v, ln_ref[0:1, :], ln_ref[1:2, :])
        x1_ref[rs, :] = x1
        h = x1 * (1.0 + mod_ref[0, 4:5, :]) + mod_ref[0, 3:4, :]
        hb = h.astype(BF16)
        hb_ref[rs, :] = hb
        h_lo = (h - hb.astype(F32)).astype(BF16)
        logits.append(lax.dot_general(wr_ref[0], hb, nt, preferred_element_type=F32)
                      + lax.dot_general(wr_ref[1], hb, nt, preferred_element_type=F32)
                      + lax.dot_general(wr_ref[0], h_lo, nt, preferred_element_type=F32))
        for e in range(n_early):
            gus[e].append(jnp.dot(hb, wgu_ref[e], preferred_element_type=F32))
    logits = jnp.concatenate(logits, axis=1)
    hb = hb_ref[...]
    x1 = x1_ref[...]
    scores = jax.nn.sigmoid(logits)
    bias = br_ref[...]
    rows = _route([scores[e:e + 1, :] for e in range(N_EXPERTS)],
                  [jnp.broadcast_to(bias[e:e + 1, :], (1, T)) for e in range(N_EXPERTS)])
    cmb = jnp.concatenate(rows + [jnp.zeros((V7X_LANES - N_EXPERTS, T), F32)], axis=0).T
    for e in range(N_EXPERTS):
        if e < n_early:
            parts = zip(halves, gus[e])
        else:
            parts = [(slice(0, T), jnp.dot(hb, wgu_ref[e], preferred_element_type=F32))]
        for rs, gu in parts:
            hid = _silu(gu[:, :F]) * gu[:, F:] * cmb[rs, e:e + 1]
            hid_ref[rs, e * F:(e + 1) * F] = hid.astype(BF16)
    for rs in (slice(0, T // 2), slice(T // 2, T)):
        y2 = jnp.dot(hid_ref[rs, :], wd_ref[...], preferred_element_type=F32)
        v2 = DEEPNORM_ALPHA * x1[rs] + mod_ref[0, 5:6, :] * y2
        o_ref[rs, :] = _layer_norm(v2, ln_ref[2:3, :], ln_ref[3:4, :])


def _sublayers(x, y, w_out, mod, mod_row, ln, w_router_t, b_router, w_gu, w_d, tile):
    n_tok, D = x.shape
    E, _, F2 = w_gu.shape
    has_proj = w_out is not None
    tok = pl.BlockSpec((tile, D), lambda t: (t, 0))
    in_specs = [tok, tok]
    args = [x, y]
    if has_proj:
        in_specs.append(_resident(w_out.shape))
        args.append(w_out)
    in_specs += [
        pl.BlockSpec((1, 6, D), lambda t: (mod_row(t), 0, 0)),
        _resident(ln.shape),
        _resident(w_router_t.shape),
        _resident((E, 1)),
        _resident(w_gu.shape),
        _resident(w_d.shape),
    ]
    args += [mod, ln, w_router_t, b_router.reshape(E, 1), w_gu, w_d]
    return pl.pallas_call(
        functools.partial(_sublayers_kernel, has_proj=has_proj),
        grid=(n_tok // tile,),
        in_specs=in_specs,
        out_specs=tok,
        out_shape=jax.ShapeDtypeStruct((n_tok, D), F32),
        scratch_shapes=[
            pltpu.VMEM((tile, E * F2 // 2), BF16),
            pltpu.VMEM((tile, D), F32),
            pltpu.VMEM((tile, D), BF16),
        ],
        compiler_params=_params(1),
    )(*args)


def _proj_kernel(x_ref, xp_ref, xn_ref, mod_ref, wqkv_ref, wz_ref, wab_ref, cw_ref, alog_ref, dtb_ref,
                 q_ref, k_ref, v_ref, z_ref, grow_ref, gcol_ref, *, tiles_per_seq):
    T = x_ref.shape[0]
    H, dk = q_ref.shape[1], q_ref.shape[3]
    hk = H * dk
    H2 = alog_ref.shape[0]
    C = DN_CHUNK
    halo = xp_ref.shape[0]
    n_taps = DN_CONV_TAPS
    scale = 1.0 + mod_ref[0, 1:2, :]
    shift = mod_ref[0, 0:1, :]
    h = x_ref[...] * scale + shift
    pos = pl.program_id(0) % tiles_per_seq
    prev_ok = (pos != 0).astype(F32)
    next_ok = (pos != tiles_per_seq - 1).astype(F32)
    h_ext = jnp.concatenate([(xp_ref[...] * scale + shift) * prev_ok, h, (xn_ref[...] * scale + shift) * next_ok],
                            axis=0)
    u = jnp.dot(h_ext.astype(BF16), wqkv_ref[...], preferred_element_type=F32)
    acc = None
    for j in range(n_taps):
        d = j - n_taps // 2
        shifted = u if d == 0 else pltpu.roll(u, (-d) % (T + 2 * halo), axis=0)
        term = shifted[halo:halo + T] * cw_ref[j:j + 1, :]
        acc = term if acc is None else acc + term
    act = _silu(acc)
    for hd in range(H):
        qh = act[:, hd * dk:(hd + 1) * dk]
        kh = act[:, hk + hd * dk:hk + (hd + 1) * dk]
        q_ref[0, hd] = (qh * (lax.rsqrt(jnp.sum(qh * qh, axis=-1, keepdims=True) + 1e-6) * dk ** -0.5)).astype(BF16)
        k_ref[0, hd] = (kh * lax.rsqrt(jnp.sum(kh * kh, axis=-1, keepdims=True) + 1e-6)).astype(BF16)
        v_ref[0, hd] = act[:, 2 * hk + hd * dk:2 * hk + (hd + 1) * dk].astype(BF16)
    h_hi = h.astype(BF16)
    z = jnp.dot(h_hi, wz_ref[...], preferred_element_type=F32)
    for hd in range(H):
        z_ref[0, hd] = z[:, hd * dk:(hd + 1) * dk].astype(BF16)
    h_lo = (h - h_hi.astype(F32)).astype(BF16)
    nt = (((1,), (1,)), ((), ()))
    ab = (lax.dot_general(wab_ref[0], h_hi, nt, preferred_element_type=F32)
          + lax.dot_general(wab_ref[1], h_hi, nt, preferred_element_type=F32)
          + lax.dot_general(wab_ref[0], h_lo, nt, preferred_element_type=F32))
    a = ab[:H2] + dtb_ref[...]
    softplus = jnp.maximum(a, 0.0) + jnp.log1p(jnp.exp(-jnp.abs(a)))
    g = -jnp.exp(alog_ref[...]) * softplus
    beta = jax.nn.sigmoid(ab[H2:])
    pos = lax.broadcasted_iota(jnp.int32, (H2, T), 1) & (C - 1)
    pre = suf = g
    s = 1
    while s < C:
        pre = pre + jnp.where(pos >= s, pltpu.roll(pre, s, axis=1), 0.0)
        suf = suf + jnp.where(pos < C - s, pltpu.roll(suf, T - s, axis=1), 0.0)
        s *= 2
    tot = pre + suf - g
    fwd_rows = lax.broadcasted_iota(jnp.int32, (H2, T), 0) < H2 // 2
    gc = jnp.where(fwd_rows, pre, suf)
    grow_ref[0] = jnp.concatenate([gc, beta], axis=0)
    n_hg, Hb = gcol_ref.shape[0], H // gcol_ref.shape[0]
    for hg in range(n_hg):
        picks = [t[d * H + hg * Hb:d * H + (hg + 1) * Hb] for t in (gc, beta, tot) for d in range(2)]
        picks.append(jnp.zeros((V7X_LANES - 6 * Hb, T), F32))
        gcol_ref[hg] = jnp.concatenate(picks, axis=0).T


def _gdn_proj(x, B, mod, mod_row, w_qkv, w_z, w_ab, conv_w, a_log, dt_bias, tile):
    n_tok, D = x.shape
    L = n_tok // B
    H = DN_HEADS
    dk = w_z.shape[1] // H
    H2 = a_log.size
    n_hg = H // _gdn_heads_per_step(L)
    w_ab_t = w_ab.T
    w_ab_hi = w_ab_t.astype(BF16)
    w_ab_split = jnp.stack([w_ab_hi, (w_ab_t - w_ab_hi.astype(F32)).astype(BF16)])
    halo = 8
    tps = L // tile
    n_halo_blocks = n_tok // halo
    head_major = pl.BlockSpec((1, H, tile, dk), lambda t: (t // tps, 0, t % tps, 0))
    qkvz = jax.ShapeDtypeStruct((B, H, L, dk), BF16)
    return pl.pallas_call(
        functools.partial(_proj_kernel, tiles_per_seq=tps),
        grid=(n_tok // tile,),
        in_specs=[
            pl.BlockSpec((tile, D), lambda t: (t, 0)),
            pl.BlockSpec((halo, D), lambda t: (jnp.maximum(t * (tile // halo) - 1, 0), 0)),
            pl.BlockSpec((halo, D), lambda t: (jnp.minimum((t + 1) * (tile // halo), n_halo_blocks - 1), 0)),
            pl.BlockSpec((1, 6, D), lambda t: (mod_row(t), 0, 0)),
            _resident(w_qkv.shape),
            _resident(w_z.shape),
            _resident(w_ab_split.shape),
            _resident(conv_w.shape),
            _resident((H2, 1)),
            _resident((H2, 1)),
        ],
        out_specs=[head_major, head_major, head_major, head_major,
                   pl.BlockSpec((1, 2 * H2, tile), lambda t: (t // tps, 0, t % tps)),
                   pl.BlockSpec((n_hg, tile, V7X_LANES), lambda t: (0, t, 0))],
        out_shape=[qkvz, qkvz, qkvz, qkvz, jax.ShapeDtypeStruct((B, 2 * H2, L), F32),
                   jax.ShapeDtypeStruct((n_hg, n_tok, V7X_LANES), F32)],
        compiler_params=_params(1),
    )(x, x, x, mod, w_qkv, w_z, w_ab_split, conv_w, a_log.reshape(H2, 1), dt_bias.reshape(H2, 1))


def _gdn_kernel(q_ref, k_ref, v_ref, z_ref, gcol_ref, grow_ref, s0f_ref, s0b_ref, ng_ref,
                o_ref, sf_ref, sb_ref,
                s_scr, wq_scr, u_scr, qk_scr, kd_scr, gl_scr, oacc_scr, *, group):
    Hb, L, dk = q_ref.shape[1], q_ref.shape[2], q_ref.shape[3]
    C = DN_CHUNK
    N = L // C
    n_groups = N // group
    gtok = group * C

    m = Hb * group
    for hh in range(Hb):
        s_scr[hh] = s0f_ref[0, hh]
        s_scr[Hb + hh] = s0b_ref[0, hh]
    oacc_scr[...] = jnp.zeros(oacc_scr.shape, F32)

    ii = lax.broadcasted_iota(jnp.int32, (C, C), 0)
    jj = lax.broadcasted_iota(jnp.int32, (C, C), 1)
    eye = (ii == jj).astype(F32)
    off_masks = []
    s = 1
    while s < C:
        same = (ii & -(2 * s)) == (jj & -(2 * s))
        lo = (same & ((ii & s) != 0) & ((jj & s) == 0)).astype(F32)
        up = (same & ((jj & s) != 0) & ((ii & s) == 0)).astype(F32)
        off_masks.append(jnp.stack([lo, up])[:, None])
        s *= 2

    def bmm(a, b):
        return jnp.einsum('nij,njd->nid', a, b, preferred_element_type=F32)

    def bmm_nt(a, b):
        return jnp.einsum('nid,njd->nij', a, b, preferred_element_type=F32)

    def heads(read):
        return jnp.concatenate([read(hh) for hh in range(Hb)], axis=0)

    def local(cgs):
        per_dir = []
        for d in range(2):
            tok = pl.ds(pl.multiple_of(cgs[d] * gtok, gtok), gtok)
            rows = pl.ds(pl.multiple_of(cgs[d] * group, group), group)
            q = heads(lambda hh: q_ref[0, hh, tok, :].reshape(group, C, dk))
            k = heads(lambda hh: k_ref[0, hh, tok, :].reshape(group, C, dk))
            v = heads(lambda hh: v_ref[0, hh, tok, :].reshape(group, C, dk))
            gcols = gcol_ref[0, tok, :]

            def col(kind):
                lane0 = kind * 2 * Hb + d * Hb
                return heads(lambda hh: gcols[:, lane0 + hh:lane0 + hh + 1].reshape(group, C, 1))

            gc, beta, gt = col(0), col(1), col(2)
            gc_row = heads(lambda hh: grow_ref[0, hh, d, rows, :])[:, None, :]
            beta_row = heads(lambda hh: grow_ref[0, hh, 2 + d, rows, :])[:, None, :]
            strict = (ii > jj) if d == 0 else (ii < jj)
            decay = jnp.exp(jnp.where(strict[None], gc - gc_row, -jnp.inf))
            a = bmm_nt(k, k) * beta * decay
            qk_scr[d * Hb:(d + 1) * Hb] = (bmm_nt(q, k) * (decay + eye[None])).astype(BF16).reshape(Hb, group, C, C)
            q_dec = (q.astype(F32) * jnp.exp(gc)).astype(BF16)
            kd_scr[d * Hb:(d + 1) * Hb] = (k.astype(F32) * jnp.exp(gt - gc)).astype(BF16).reshape(Hb, group, C, dk)
            gl_scr[d * Hb:(d + 1) * Hb] = jnp.broadcast_to(jnp.exp(gt[:, 0:1, :]), (m, 8, dk)).reshape(Hb, group, 8, dk)
            per_dir.append((a, k, v, q_dec, gc_row, beta_row))
        a4 = jnp.stack([per_dir[0][0], per_dir[1][0]])
        t_inv = eye[None] - (a4 * off_masks[0]).reshape(2 * m, C, C)
        for mask in off_masks[1:]:
            t_b = t_inv.astype(BF16)
            a_off = (a4 * mask).reshape(2 * m, C, C).astype(BF16)
            t_inv = t_inv - bmm(t_b, bmm(a_off, t_b).astype(BF16))
        for d in range(2):
            _, k, v, q_dec, gc_row, beta_row = per_dir[d]
            t_beta = t_inv[d * m:(d + 1) * m] * beta_row
            u_scr[d * Hb:(d + 1) * Hb] = bmm(t_beta.astype(BF16), v).reshape(Hb, group, C, dk)
            w = bmm((t_beta * jnp.exp(gc_row)).astype(BF16), k)
            wq_scr[d * Hb:(d + 1) * Hb] = jnp.concatenate([w.astype(BF16), q_dec], axis=1).reshape(Hb, group, 2 * C, dk)

    def step(cgs, i):
        slots = [(d, hh) for d in range(2) for hh in range(Hb)]
        ns = [i if d == 0 else group - 1 - i for d, _ in slots]
        states = [s_scr[c] for c in range(2 * Hb)]
        r = [jnp.dot(wq_scr[c, ns[c]], states[c].astype(BF16), preferred_element_type=F32)
             for c in range(2 * Hb)]
        v_new = [(u_scr[c, ns[c]] - r[c][:C]).astype(BF16) for c in range(2 * Hb)]
        o = [jnp.dot(qk_scr[c, ns[c]], v_new[c], preferred_element_type=F32) for c in range(2 * Hb)]
        upd = [lax.dot_general(kd_scr[c, ns[c]], v_new[c], (((0,), (0,)), ((), ())), preferred_element_type=F32)
               for c in range(2 * Hb)]
        for c, (d, hh) in enumerate(slots):
            s_scr[c] = states[c] * gl_scr[c, ns[c]][0:1, :] + upd[c]
            rows = pl.ds(pl.multiple_of((cgs[d] * group + ns[c]) * C, C), C)
            oacc_scr[hh, rows, :] += r[c][C:] + o[c]

    def group_body(gi, _):
        cgs = (gi, n_groups - 1 - gi)
        local(cgs)
        lax.fori_loop(0, group, lambda i, _: (step(cgs, i), 0)[1], 0)
        return 0

    lax.fori_loop(0, n_groups, group_body, 0)

    for hh in range(Hb):
        sf_ref[0, hh] = s_scr[hh]
        sb_ref[0, hh] = s_scr[Hb + hh]
        o = oacc_scr[hh]
        o = o * lax.rsqrt(jnp.mean(o * o, axis=-1, keepdims=True) + 1e-6) * ng_ref[...]
        o_ref[0, :, hh * dk:(hh + 1) * dk] = (o * _silu(z_ref[0, hh].astype(F32))).astype(BF16)


def _gdn_core(q, k, v, z, gcol, grow, s0f, s0b, norm_g):
    B, H, L, dk = q.shape
    C = DN_CHUNK
    N = L // C
    group = min(N, GDN_GROUP_CHUNKS)
    Hb = H // gcol.shape[0]
    n_chains = 2 * Hb
    seq = pl.BlockSpec((1, Hb, L, dk), lambda b, h: (b, h, 0, 0))
    st = pl.BlockSpec((1, Hb, dk, dk), lambda b, h: (b, h, 0, 0))
    return pl.pallas_call(
        functools.partial(_gdn_kernel, group=group),
        grid=(B, H // Hb),
        in_specs=[
            seq, seq, seq, seq,
            pl.BlockSpec((1, L, V7X_LANES), lambda b, h: (h, b, 0)),
            pl.BlockSpec((1, Hb, 4, N, C), lambda b, h: (b, h, 0, 0, 0)),
            st, st,
            pl.BlockSpec((1, dk), lambda b, h: (0, 0)),
        ],
        out_specs=[pl.BlockSpec((1, L, Hb * dk), lambda b, h: (b, 0, h)), st, st],
        out_shape=[
            jax.ShapeDtypeStruct((B, L, H * dk), BF16),
            jax.ShapeDtypeStruct((B, H, dk, dk), F32),
            jax.ShapeDtypeStruct((B, H, dk, dk), F32),
        ],
        scratch_shapes=[
            pltpu.VMEM((n_chains, dk, dk), F32),
            pltpu.VMEM((n_chains, group, 2 * C, dk), BF16),
            pltpu.VMEM((n_chains, group, C, dk), F32),
            pltpu.VMEM((n_chains, group, C, C), BF16),
            pltpu.VMEM((n_chains, group, C, dk), BF16),
            pltpu.VMEM((n_chains, group, 8, dk), F32),
            pltpu.VMEM((Hb, L, dk), F32),
        ],
        compiler_params=_params(2),
    )(q, k, v, z, gcol, grow, s0f, s0b, norm_g.reshape(1, dk))


def _gdn_mixer(x_tok, B, mod, mod_row, w_qkv, w_z, w_ab, conv_w, a_log, dt_bias, norm_g, s0f, s0b, tile):
    n_tok = x_tok.shape[0]
    L = n_tok // B
    H, C = DN_HEADS, DN_CHUNK
    q, k, v, z, grows, gcol = _gdn_proj(x_tok, B, mod, mod_row, w_qkv, w_z, w_ab, conv_w, a_log, dt_bias, tile)
    grow = grows.reshape(B, 2, 2, H, L).transpose(0, 3, 1, 2, 4).reshape(B, H, 4, L // C, C)
    o, s_f, s_b = _gdn_core(q, k, v, z, gcol, grow, s0f, s0b, norm_g)
    return o.reshape(n_tok, -1), s_f, s_b


def kernel(x, c, ctx, c_ctx, w_mod, b_mod, ln_g, ln_b, pool_w, pool_scale, dn_w_in, dn_conv, dn_a_log, dn_dt_bias, dn_norm, dn_w_out, w_router, b_router, w_gate, w_up, w_down):
    B, L, D = x.shape
    Lc = ctx.shape[1]
    E = w_router.shape[1]
    F = w_gate.shape[-1]
    H = DN_HEADS
    dk = dn_norm.shape[-1]
    hk = H * dk
    ctx_row = B

    n_rows = -(-(B + 1) // 8) * 8
    cvecs = jnp.concatenate([c, c_ctx[None, :], jnp.zeros((n_rows - B - 1, D), F32)], axis=0)
    mods = _modulation(cvecs, w_mod, b_mod)
    ln = jnp.stack([ln_g, ln_b], axis=2).reshape(ln_g.shape[0], 4, D)
    w_router_hi = w_router.T.astype(BF16)
    w_router_t = jnp.stack([w_router_hi, (w_router.T - w_router_hi.astype(F32)).astype(BF16)])
    tile = 512
    x_row = lambda t: t // (L // tile)
    c_row = lambda t: ctx_row

    mod = mods[0]
    w_gu = jnp.concatenate([w_gate[0], w_up[0]], axis=-1).astype(BF16)
    w_d = w_down[0].reshape(E * F, D).astype(BF16)
    streams = []
    for tok, rows, row_of_batch, row_of_tile in ((x, L // GRID_W, lambda b: b, x_row), (ctx, None, lambda b: ctx_row, c_row)):
        y = _pool_mixer(tok, mod, row_of_batch, pool_w[0], pool_scale[0], rows)
        streams.append(_sublayers(tok.reshape(-1, D), y.reshape(-1, D), None, mod, row_of_tile, ln[0], w_router_t,
                                  b_router, w_gu, w_d, tile))
    x_tok, ctx_tok = streams

    mod = mods[1]
    w_in = dn_w_in[0]
    w_qkv = w_in[:, :3 * hk].astype(BF16)
    w_z = w_in[:, 3 * hk:4 * hk].astype(BF16)
    w_ab = w_in[:, 4 * hk:]
    conv_w = jnp.concatenate([dn_conv[0], jnp.zeros((8 - dn_conv.shape[1], 3 * hk), F32)], axis=0)
    a_log, dt_bias, norm_g = dn_a_log[0], dn_dt_bias[0], dn_norm[0]
    zeros = jnp.zeros((B, H, dk, dk), F32)
    _, s_f, s_b = _gdn_mixer(ctx_tok, B, mod, c_row, w_qkv, w_z, w_ab, conv_w, a_log, dt_bias, norm_g, zeros, zeros,
                             min(tile, Lc))
    o, _, _ = _gdn_mixer(x_tok, B, mod, x_row, w_qkv, w_z, w_ab, conv_w, a_log, dt_bias, norm_g, s_f, s_b, tile)
    w_gu = jnp.concatenate([w_gate[1], w_up[1]], axis=-1).astype(BF16)
    w_d = w_down[1].reshape(E * F, D).astype(BF16)
    out = _sublayers(x_tok, o, dn_w_out[0].astype(BF16), mod, x_row, ln[1], w_router_t, b_router, w_gu, w_d, tile)
    return out.reshape(B, L, D)
```

```python
import functools

import numpy as np
import jax
import jax.numpy as jnp
from jax import lax
from jax.experimental import pallas as pl
from jax.experimental.pallas import tpu as pltpu

GRID_W = 64
POOL_WINDOWS = (2, 4, 8, 16)
DN_HEADS = 8
DN_CHUNK = 64
DN_CONV_TAPS = 5
N_GROUPS = 4
EXPERTS_PER_GROUP = 4
N_EXPERTS = N_GROUPS * EXPERTS_PER_GROUP
DEPTH = 2
DEEPNORM_ALPHA = (2.0 * DEPTH) ** 0.25
LN_EPS = 1e-5

V7X_LANES = 128
V7X_VMEM_LIMIT_BYTES = 56 * 1024 * 1024

GDN_GROUP_CHUNKS = 8
GDN_GROUPS_PER_ITERATION = 1


def _gdn_heads_per_step(seq_len):
    return DN_HEADS if seq_len <= GDN_GROUP_CHUNKS * DN_CHUNK else DN_HEADS // 2

F32 = jnp.float32
BF16 = jnp.bfloat16
HIGHEST = lax.Precision.HIGHEST


def _params(n_grid_dims):
    return pltpu.CompilerParams(
        dimension_semantics=("arbitrary",) * n_grid_dims,
        vmem_limit_bytes=V7X_VMEM_LIMIT_BYTES,
    )


def _resident(shape):
    zeros = (0,) * len(shape)
    return pl.BlockSpec(shape, lambda *_: zeros, pipeline_mode=pl.Buffered(1))


def _silu(x):
    return x * jax.nn.sigmoid(x)


def _layer_norm(v, g, b):
    mu = jnp.mean(v, axis=-1, keepdims=True)
    cen = v - mu
    var = jnp.mean(cen * cen, axis=-1, keepdims=True)
    return cen * lax.rsqrt(var + LN_EPS) * g + b


def _mod_kernel(c_ref, w_ref, b_ref, o_ref):
    s = _silu(c_ref[...])
    o_ref[0] = jnp.dot(s, w_ref[0], precision=HIGHEST, preferred_element_type=F32) + b_ref[0]


def _modulation(cvecs, w_mod, b_mod):
    R, D = cvecs.shape
    depth, _, six_d = w_mod.shape
    tn = 1536
    out = pl.pallas_call(
        _mod_kernel,
        grid=(depth, six_d // tn),
        in_specs=[
            pl.BlockSpec((R, D), lambda i, n: (0, 0)),
            pl.BlockSpec((1, D, tn), lambda i, n: (i, 0, n)),
            pl.BlockSpec((1, 1, tn), lambda i, n: (i, 0, n)),
        ],
        out_specs=pl.BlockSpec((1, R, tn), lambda i, n: (i, 0, n)),
        out_shape=jax.ShapeDtypeStruct((depth, R, six_d), F32),
        compiler_params=_params(2),
    )(cvecs, w_mod, b_mod.reshape(depth, 1, six_d))
    return out.reshape(depth, R, 6, D)


def _pool_kernel(x_ref, mod_ref, band_ref, icnt_ref, w_ref, ps_ref, o_ref, *, grid_rows):
    L, Cg = x_ref.shape[1], x_ref.shape[2]
    slab = band_ref.shape[1]
    g = pl.program_id(1)

    def body(k):
        h = x_ref[0] * (1.0 + mod_ref[0, 1:2, :]) + mod_ref[0, 0:1, :]
        if grid_rows is None:
            s = h
        else:
            r3 = h.reshape(grid_rows, L // grid_rows, Cg)
            zpad = jnp.zeros((k // 2,) + r3.shape[1:], F32)
            win = jnp.concatenate([zpad, r3, zpad], axis=0)
            span = 1
            while span < k:
                win = win[:-span] + win[span:]
                span *= 2
            s = win[:grid_rows].reshape(L, Cg)
        band = band_ref[0]
        w = w_ref[0].astype(BF16)
        for i in range(L // slab):
            sl = slice(i * slab, (i + 1) * slab)
            s_i = s[sl]
            hi = s_i.astype(BF16)
            lo = (s_i - hi.astype(F32)).astype(BF16)
            tot = jnp.dot(band, hi, preferred_element_type=F32) + jnp.dot(band, lo, preferred_element_type=F32)
            ic = icnt_ref[0, sl, :]
            mean = tot * jnp.concatenate([ic] * (Cg // V7X_LANES), axis=-1)
            p = mean - h[sl]
            y = jnp.dot(p.astype(BF16), w, preferred_element_type=F32)
            o_ref[0, sl, :] = y * ps_ref[...]

    if grid_rows is None:
        body(None)
    else:
        for gi, k in enumerate(POOL_WINDOWS):
            pl.when(g == gi)(functools.partial(body, k))


def _pool_tables(L, grid_rows, slab):
    n_g = len(POOL_WINDOWS)
    band = np.zeros((n_g, slab, slab), np.float32)
    icnt = np.zeros((n_g, L), np.float32)
    t = np.arange(L)
    for gi, k in enumerate(POOL_WINDOWS):
        if grid_rows is None:
            pos, n, blk = np.arange(slab), L, np.zeros(slab, np.int64)
            cnt = np.clip(t + k - k // 2, 0, L) - np.clip(t - k // 2, 0, L)
        else:
            w = L // grid_rows
            pos, n, blk = np.arange(slab) % w, w, np.arange(slab) // w
            row, col = t // w, t % w
            cnt_r = np.clip(row + k - k // 2, 0, grid_rows) - np.clip(row - k // 2, 0, grid_rows)
            cnt_c = np.clip(col + k - k // 2, 0, w) - np.clip(col - k // 2, 0, w)
            cnt = cnt_r * cnt_c
        lo = np.clip(pos - k // 2, 0, n)
        hi = np.clip(pos + k - k // 2, 0, n)
        inside = (pos[None, :] >= lo[:, None]) & (pos[None, :] < hi[:, None]) & (blk[None, :] == blk[:, None])
        band[gi] = inside.astype(np.float32)
        icnt[gi] = 1.0 / cnt
    icnt = np.broadcast_to(icnt[:, :, None], (n_g, L, V7X_LANES))
    return jnp.asarray(band, BF16), jnp.asarray(icnt, F32)


def _pool_mixer(x, mod, mod_row, pool_w, pool_scale, grid_rows):
    B, L, D = x.shape
    n_g = len(POOL_WINDOWS)
    Cg = D // n_g
    slab = 256
    assert L % slab == 0 and (grid_rows is None or (L == slab or slab % (L // grid_rows) == 0))
    if grid_rows is None:
        assert L == slab
    band, icnt = _pool_tables(L, grid_rows, slab)
    return pl.pallas_call(
        functools.partial(_pool_kernel, grid_rows=grid_rows),
        grid=(B, n_g),
        in_specs=[
            pl.BlockSpec((1, L, Cg), lambda b, g: (b, 0, g)),
            pl.BlockSpec((1, 6, Cg), lambda b, g: (mod_row(b), 0, g)),
            pl.BlockSpec((1, slab, slab), lambda b, g: (g, 0, 0)),
            pl.BlockSpec((1, L, V7X_LANES), lambda b, g: (g, 0, 0)),
            pl.BlockSpec((1, Cg, Cg), lambda b, g: (g, 0, 0)),
            pl.BlockSpec((1, Cg), lambda b, g: (0, g)),
        ],
        out_specs=pl.BlockSpec((1, L, Cg), lambda b, g: (b, 0, g)),
        out_shape=jax.ShapeDtypeStruct((B, L, D), F32),
        compiler_params=_params(2),
    )(x, mod, band, icnt, pool_w, pool_scale.reshape(1, D))


def _route(scores, bias):
    sel = [s + b for s, b in zip(scores, bias)]
    E = EXPERTS_PER_GROUP
    grp = []
    for gi in range(N_GROUPS):
        v = sel[gi * E:(gi + 1) * E]
        best = None
        for a in range(E):
            for c in range(a + 1, E):
                pair = v[a] + v[c]
                best = pair if best is None else jnp.maximum(best, pair)
        grp.append(best)
    g_idx = jnp.zeros_like(grp[0], dtype=jnp.int32)
    g_best = grp[0]
    for gi in range(1, N_GROUPS):
        better = grp[gi] > g_best
        g_idx = jnp.where(better, gi, g_idx)
        g_best = jnp.where(better, grp[gi], g_best)
    in_sel, in_score = [], []
    for l in range(E):
        vs, vc = sel[l], scores[l]
        for gi in range(1, N_GROUPS):
            vs = jnp.where(g_idx == gi, sel[gi * E + l], vs)
            vc = jnp.where(g_idx == gi, scores[gi * E + l], vc)
        in_sel.append(vs)
        in_score.append(vc)
    i1 = jnp.zeros_like(g_idx)
    m1 = in_sel[0]
    for l in range(1, E):
        better = in_sel[l] > m1
        i1 = jnp.where(better, l, i1)
        m1 = jnp.where(better, in_sel[l], m1)
    i2 = jnp.full_like(g_idx, -1)
    m2 = jnp.full_like(m1, -jnp.inf)
    for l in range(E):
        better = (i1 != l) & ((in_sel[l] > m2) | (i2 < 0))
        i2 = jnp.where(better, l, i2)
        m2 = jnp.where(better, in_sel[l], m2)
    s1 = in_score[0]
    s2 = in_score[0]
    for l in range(1, E):
        s1 = jnp.where(i1 == l, in_score[l], s1)
        s2 = jnp.where(i2 == l, in_score[l], s2)
    tot = s1 + s2
    w1, w2 = s1 / tot, s2 / tot
    e1 = g_idx * E + i1
    e2 = g_idx * E + i2
    return [jnp.where(e1 == e, w1, 0.0) + jnp.where(e2 == e, w2, 0.0) for e in range(N_EXPERTS)]


def _sublayers_kernel(*refs, has_proj):
    if has_proj:
        x_ref, y_ref, wo_ref, mod_ref, ln_ref, wr_ref, br_ref, wgu_ref, wd_ref, o_ref, hid_ref, x1_ref, hb_ref = refs
    else:
        x_ref, y_ref, mod_ref, ln_ref, wr_ref, br_ref, wgu_ref, wd_ref, o_ref, hid_ref, x1_ref, hb_ref = refs
    T = x_ref.shape[0]
    F = wgu_ref.shape[2] // 2
    halves = (slice(0, T // 2), slice(T // 2, T))
    nt = (((1,), (1,)), ((), ()))
    n_early = 2
    logits, gus = [], [[] for _ in range(n_early)]
    y_full = jnp.dot(y_ref[...], wo_ref[...], preferred_element_type=F32) if has_proj else None
    for rs in halves:
        y = y_full[rs] if has_proj else y_ref[rs, :]
        v = DEEPNORM_ALPHA * x_ref[rs, :] + mod_ref[0, 2:3, :] * y
        x1 = _layer_norm(v, ln_ref[0:1, :], ln_ref[1:2, :])
        x1_ref[rs, :] = x1
        h = x1 * (1.0 + mod_ref[0, 4:5, :]) + mod_ref[0, 3:4, :]
        hb = h.astype(BF16)
        hb_ref[rs, :] = hb
        h_lo = (h - hb.astype(F32)).astype(BF16)
        logits.append(lax.dot_general(wr_ref[0], hb, nt, preferred_element_type=F32)
                      + lax.dot_general(wr_ref[1], hb, nt, preferred_element_type=F32)
                      + lax.dot_general(wr_ref[0], h_lo, nt, preferred_element_type=F32))
        for e in range(n_early):
            gus[e].append(jnp.dot(hb, wgu_ref[e], preferred_element_type=F32))
    logits = jnp.concatenate(logits, axis=1)
    hb = hb_ref[...]
    x1 = x1_ref[...]
    scores = jax.nn.sigmoid(logits)
    bias = br_ref[...]
    rows = _route([scores[e:e + 1, :] for e in range(N_EXPERTS)],
                  [jnp.broadcast_to(bias[e:e + 1, :], (1, T)) for e in range(N_EXPERTS)])
    cmb = jnp.concatenate(rows + [jnp.zeros((V7X_LANES - N_EXPERTS, T), F32)], axis=0).T
    for e in range(N_EXPERTS):
        if e < n_early:
            parts = zip(halves, gus[e])
        else:
            parts = [(slice(0, T), jnp.dot(hb, wgu_ref[e], preferred_element_type=F32))]
        for rs, gu in parts:
            hid = _silu(gu[:, :F]) * gu[:, F:] * cmb[rs, e:e + 1]
            hid_ref[rs, e * F:(e + 1) * F] = hid.astype(BF16)
    for rs in (slice(0, T // 2), slice(T // 2, T)):
        y2 = jnp.dot(hid_ref[rs, :], wd_ref[...], preferred_element_type=F32)
        v2 = DEEPNORM_ALPHA * x1[rs] + mod_ref[0, 5:6, :] * y2
        o_ref[rs, :] = _layer_norm(v2, ln_ref[2:3, :], ln_ref[3:4, :])


def _sublayers(x, y, w_out, mod, mod_row, ln, w_router_t, b_router, w_gu, w_d, tile):
    n_tok, D = x.shape
    E, _, F2 = w_gu.shape
    has_proj = w_out is not None
    tok = pl.BlockSpec((tile, D), lambda t: (t, 0))
    in_specs = [tok, tok]
    args = [x, y]
    if has_proj:
        in_specs.append(_resident(w_out.shape))
        args.append(w_out)
    in_specs += [
        pl.BlockSpec((1, 6, D), lambda t: (mod_row(t), 0, 0)),
        _resident(ln.shape),
        _resident(w_router_t.shape),
        _resident((E, 1)),
        _resident(w_gu.shape),
        _resident(w_d.shape),
    ]
    args += [mod, ln, w_router_t, b_router.reshape(E, 1), w_gu, w_d]
    return pl.pallas_call(
        functools.partial(_sublayers_kernel, has_proj=has_proj),
        grid=(n_tok // tile,),
        in_specs=in_specs,
        out_specs=tok,
        out_shape=jax.ShapeDtypeStruct((n_tok, D), F32),
        scratch_shapes=[
            pltpu.VMEM((tile, E * F2 // 2), BF16),
            pltpu.VMEM((tile, D), F32),
            pltpu.VMEM((tile, D), BF16),
        ],
        compiler_params=_params(1),
    )(*args)


def _proj_kernel(x_ref, xp_ref, xn_ref, mod_ref, wqkv_ref, wz_ref, wab_ref, cw_ref, alog_ref, dtb_ref,
                 q_ref, k_ref, v_ref, z_ref, grow_ref, gcol_ref, *, tiles_per_seq):
    T = x_ref.shape[0]
    H, dk = q_ref.shape[1], q_ref.shape[3]
    hk = H * dk
    H2 = alog_ref.shape[0]
    C = DN_CHUNK
    halo = xp_ref.shape[0]
    n_taps = DN_CONV_TAPS
    scale = 1.0 + mod_ref[0, 1:2, :]
    shift = mod_ref[0, 0:1, :]
    h = x_ref[...] * scale + shift
    pos = pl.program_id(0) % tiles_per_seq
    prev_ok = (pos != 0).astype(F32)
    next_ok = (pos != tiles_per_seq - 1).astype(F32)
    h_ext = jnp.concatenate([(xp_ref[...] * scale + shift) * prev_ok, h, (xn_ref[...] * scale + shift) * next_ok],
                            axis=0)
    u = jnp.dot(h_ext.astype(BF16), wqkv_ref[...], preferred_element_type=F32)
    acc = None
    for j in range(n_taps):
        d = j - n_taps // 2
        shifted = u if d == 0 else pltpu.roll(u, (-d) % (T + 2 * halo), axis=0)
        term = shifted[halo:halo + T] * cw_ref[j:j + 1, :]
        acc = term if acc is None else acc + term
    act = _silu(acc)
    for hd in range(H):
        qh = act[:, hd * dk:(hd + 1) * dk]
        kh = act[:, hk + hd * dk:hk + (hd + 1) * dk]
        q_ref[0, hd] = (qh * (lax.rsqrt(jnp.sum(qh * qh, axis=-1, keepdims=True) + 1e-6) * dk ** -0.5)).astype(BF16)
        k_ref[0, hd] = (kh * lax.rsqrt(jnp.sum(kh * kh, axis=-1, keepdims=True) + 1e-6)).astype(BF16)
        v_ref[0, hd] = act[:, 2 * hk + hd * dk:2 * hk + (hd + 1) * dk].astype(BF16)
    h_hi = h.astype(BF16)
    z = jnp.dot(h_hi, wz_ref[...], preferred_element_type=F32)
    for hd in range(H):
        z_ref[0, hd] = z[:, hd * dk:(hd + 1) * dk].astype(BF16)
    h_lo = (h - h_hi.astype(F32)).astype(BF16)
    nt = (((1,), (1,)), ((), ()))
    ab = (lax.dot_general(wab_ref[0], h_hi, nt, preferred_element_type=F32)
          + lax.dot_general(wab_ref[1], h_hi, nt, preferred_element_type=F32)
          + lax.dot_general(wab_ref[0], h_lo, nt, preferred_element_type=F32))
    a = ab[:H2] + dtb_ref[...]
    softplus = jnp.maximum(a, 0.0) + jnp.log1p(jnp.exp(-jnp.abs(a)))
    g = -jnp.exp(alog_ref[...]) * softplus
    beta = jax.nn.sigmoid(ab[H2:])
    pos = lax.broadcasted_iota(jnp.int32, (H2, T), 1) & (C - 1)
    pre = suf = g
    s = 1
    while s < C:
        pre = pre + jnp.where(pos >= s, pltpu.roll(pre, s, axis=1), 0.0)
        suf = suf + jnp.where(pos < C - s, pltpu.roll(suf, T - s, axis=1), 0.0)
        s *= 2
    tot = pre + suf - g
    fwd_rows = lax.broadcasted_iota(jnp.int32, (H2, T), 0) < H2 // 2
    gc = jnp.where(fwd_rows, pre, suf)
    grow_ref[0] = jnp.concatenate([gc, beta], axis=0)
    n_hg, Hb = gcol_ref.shape[0], H // gcol_ref.shape[0]
    for hg in range(n_hg):
        picks = [t[d * H + hg * Hb:d * H + (hg + 1) * Hb] for t in (gc, beta, tot) for d in range(2)]
        picks.append(jnp.zeros((V7X_LANES - 6 * Hb, T), F32))
        gcol_ref[hg] = jnp.concatenate(picks, axis=0).T


def _gdn_proj(x, B, mod, mod_row, w_qkv, w_z, w_ab, conv_w, a_log, dt_bias, tile):
    n_tok, D = x.shape
    L = n_tok // B
    H = DN_HEADS
    dk = w_z.shape[1] // H
    H2 = a_log.size
    n_hg = H // _gdn_heads_per_step(L)
    w_ab_t = w_ab.T
    w_ab_hi = w_ab_t.astype(BF16)
    w_ab_split = jnp.stack([w_ab_hi, (w_ab_t - w_ab_hi.astype(F32)).astype(BF16)])
    halo = 8
    tps = L // tile
    n_halo_blocks = n_tok // halo
    head_major = pl.BlockSpec((1, H, tile, dk), lambda t: (t // tps, 0, t % tps, 0))
    qkvz = jax.ShapeDtypeStruct((B, H, L, dk), BF16)
    return pl.pallas_call(
        functools.partial(_proj_kernel, tiles_per_seq=tps),
        grid=(n_tok // tile,),
        in_specs=[
            pl.BlockSpec((tile, D), lambda t: (t, 0)),
            pl.BlockSpec((halo, D), lambda t: (jnp.maximum(t * (tile // halo) - 1, 0), 0)),
            pl.BlockSpec((halo, D), lambda t: (jnp.minimum((t + 1) * (tile // halo), n_halo_blocks - 1), 0)),
            pl.BlockSpec((1, 6, D), lambda t: (mod_row(t), 0, 0)),
            _resident(w_qkv.shape),
            _resident(w_z.shape),
            _resident(w_ab_split.shape),
            _resident(conv_w.shape),
            _resident((H2, 1)),
            _resident((H2, 1)),
        ],
        out_specs=[head_major, head_major, head_major, head_major,
                   pl.BlockSpec((1, 2 * H2, tile), lambda t: (t // tps, 0, t % tps)),
                   pl.BlockSpec((n_hg, tile, V7X_LANES), lambda t: (0, t, 0))],
        out_shape=[qkvz, qkvz, qkvz, qkvz, jax.ShapeDtypeStruct((B, 2 * H2, L), F32),
                   jax.ShapeDtypeStruct((n_hg, n_tok, V7X_LANES), F32)],
        compiler_params=_params(1),
    )(x, x, x, mod, w_qkv, w_z, w_ab_split, conv_w, a_log.reshape(H2, 1), dt_bias.reshape(H2, 1))


def _gdn_kernel(q_ref, k_ref, v_ref, z_ref, gcol_ref, grow_ref, s0f_ref, s0b_ref, ng_ref,
                o_ref, sf_ref, sb_ref,
                s_scr, wq_scr, u_scr, qk_scr, kd_scr, gl_scr, oacc_scr, *, group):
    Hb, L, dk = q_ref.shape[1], q_ref.shape[2], q_ref.shape[3]
    C = DN_CHUNK
    N = L // C
    n_groups = N // group
    gtok = group * C

    m = Hb * group
    for hh in range(Hb):
        s_scr[hh] = s0f_ref[0, hh]
        s_scr[Hb + hh] = s0b_ref[0, hh]
    oacc_scr[...] = jnp.zeros(oacc_scr.shape, F32)

    ii = lax.broadcasted_iota(jnp.int32, (C, C), 0)
    jj = lax.broadcasted_iota(jnp.int32, (C, C), 1)
    eye = (ii == jj).astype(F32)
    off_masks = []
    s = 1
    while s < C:
        same = (ii & -(2 * s)) == (jj & -(2 * s))
        lo = (same & ((ii & s) != 0) & ((jj & s) == 0)).astype(F32)
        up = (same & ((jj & s) != 0) & ((ii & s) == 0)).astype(F32)
        off_masks.append(jnp.stack([lo, up])[:, None])
        s *= 2

    def bmm(a, b):
        return jnp.einsum('nij,njd->nid', a, b, preferred_element_type=F32)

    def bmm_nt(a, b):
        return jnp.einsum('nid,njd->nij', a, b, preferred_element_type=F32)

    def heads(read):
        return jnp.concatenate([read(hh) for hh in range(Hb)], axis=0)

    def local(cgs, par):
        per_dir = []
        for d in range(2):
            tok = pl.ds(pl.multiple_of(cgs[d] * gtok, gtok), gtok)
            rows = pl.ds(pl.multiple_of(cgs[d] * group, group), group)
            q = heads(lambda hh: q_ref[0, hh, tok, :].reshape(group, C, dk))
            k = heads(lambda hh: k_ref[0, hh, tok, :].reshape(group, C, dk))
            v = heads(lambda hh: v_ref[0, hh, tok, :].reshape(group, C, dk))
            gcols = gcol_ref[0, tok, :]

            def col(kind):
                lane0 = kind * 2 * Hb + d * Hb
                return heads(lambda hh: gcols[:, lane0 + hh:lane0 + hh + 1].reshape(group, C, 1))

            gc, beta, gt = col(0), col(1), col(2)
            gc_row = heads(lambda hh: grow_ref[0, hh, d, rows, :])[:, None, :]
            beta_row = heads(lambda hh: grow_ref[0, hh, 2 + d, rows, :])[:, None, :]
            strict = (ii > jj) if d == 0 else (ii < jj)
            decay = jnp.exp(jnp.where(strict[None], gc - gc_row, -jnp.inf))
            a = bmm_nt(k, k) * beta * decay
            chains = slice(d * Hb, (d + 1) * Hb)
            qk_scr[par, chains] = (bmm_nt(q, k) * (decay + eye[None])).astype(BF16).reshape(Hb, group, C, C)
            q_dec = (q.astype(F32) * jnp.exp(gc)).astype(BF16)
            kd_scr[par, chains] = (k.astype(F32) * jnp.exp(gt - gc)).astype(BF16).reshape(Hb, group, C, dk)
            gl_scr[par, chains] = jnp.broadcast_to(jnp.exp(gt[:, 0:1, :]), (m, 8, dk)).reshape(Hb, group, 8, dk)
            per_dir.append((a, k, v, q_dec, gc_row, beta_row))
            yield
        a4 = jnp.stack([per_dir[0][0], per_dir[1][0]])
        t_inv = eye[None] - (a4 * off_masks[0]).reshape(2 * m, C, C)
        for mask in off_masks[1:]:
            t_b = t_inv.astype(BF16)
            a_off = (a4 * mask).reshape(2 * m, C, C).astype(BF16)
            t_inv = t_inv - bmm(t_b, bmm(a_off, t_b).astype(BF16))
            yield
        for d in range(2):
            _, k, v, q_dec, gc_row, beta_row = per_dir[d]
            chains = slice(d * Hb, (d + 1) * Hb)
            t_beta = t_inv[d * m:(d + 1) * m] * beta_row
            u_scr[par, chains] = bmm(t_beta.astype(BF16), v).reshape(Hb, group, C, dk)
            w = bmm((t_beta * jnp.exp(gc_row)).astype(BF16), k)
            wq_scr[par, chains] = jnp.concatenate([w.astype(BF16), q_dec], axis=1).reshape(Hb, group, 2 * C, dk)
            yield

    def step(cgs, par, i):
        slots = [(d, hh) for d in range(2) for hh in range(Hb)]
        ns = [i if d == 0 else group - 1 - i for d, _ in slots]
        states = [s_scr[c] for c in range(2 * Hb)]
        r = [jnp.dot(wq_scr[par, c, ns[c]], states[c].astype(BF16), preferred_element_type=F32)
             for c in range(2 * Hb)]
        v_new = [(u_scr[par, c, ns[c]] - r[c][:C]).astype(BF16) for c in range(2 * Hb)]
        o = [jnp.dot(qk_scr[par, c, ns[c]], v_new[c], preferred_element_type=F32) for c in range(2 * Hb)]
        upd = [lax.dot_general(kd_scr[par, c, ns[c]], v_new[c], (((0,), (0,)), ((), ())),
                               preferred_element_type=F32) for c in range(2 * Hb)]
        for c, (d, hh) in enumerate(slots):
            s_scr[c] = states[c] * gl_scr[par, c, ns[c]][0:1, :] + upd[c]
            rows = pl.ds(pl.multiple_of((cgs[d] * group + ns[c]) * C, C), C)
            oacc_scr[hh, rows, :] += r[c][C:] + o[c]

    span = min(n_groups, GDN_GROUPS_PER_ITERATION)
    group_of = lambda g: (g, n_groups - 1 - g)

    def run_span(it, _):
        g0 = it * span
        for _ in local(group_of(g0), 0):
            pass
        for j in range(span):
            ahead = local(group_of(g0 + j + 1), (j + 1) % 2) if j + 1 < span else iter(())
            for i in range(group):
                step(group_of(g0 + j), j % 2, i)
                next(ahead, None)
            for _ in ahead:
                pass
        return 0

    lax.fori_loop(0, n_groups // span, run_span, 0)

    for hh in range(Hb):
        sf_ref[0, hh] = s_scr[hh]
        sb_ref[0, hh] = s_scr[Hb + hh]
        o = oacc_scr[hh]
        o = o * lax.rsqrt(jnp.mean(o * o, axis=-1, keepdims=True) + 1e-6) * ng_ref[...]
        o_ref[0, :, hh * dk:(hh + 1) * dk] = (o * _silu(z_ref[0, hh].astype(F32))).astype(BF16)


def _gdn_core(q, k, v, z, gcol, grow, s0f, s0b, norm_g):
    B, H, L, dk = q.shape
    C = DN_CHUNK
    N = L // C
    group = min(N, GDN_GROUP_CHUNKS)
    Hb = H // gcol.shape[0]
    n_chains = 2 * Hb
    seq = pl.BlockSpec((1, Hb, L, dk), lambda b, h: (b, h, 0, 0))
    st = pl.BlockSpec((1, Hb, dk, dk), lambda b, h: (b, h, 0, 0))
    return pl.pallas_call(
        functools.partial(_gdn_kernel, group=group),
        grid=(B, H // Hb),
        in_specs=[
            seq, seq, seq, seq,
            pl.BlockSpec((1, L, V7X_LANES), lambda b, h: (h, b, 0)),
            pl.BlockSpec((1, Hb, 4, N, C), lambda b, h: (b, h, 0, 0, 0)),
            st, st,
            pl.BlockSpec((1, dk), lambda b, h: (0, 0)),
        ],
        out_specs=[pl.BlockSpec((1, L, Hb * dk), lambda b, h: (b, 0, h)), st, st],
        out_shape=[
            jax.ShapeDtypeStruct((B, L, H * dk), BF16),
            jax.ShapeDtypeStruct((B, H, dk, dk), F32),
            jax.ShapeDtypeStruct((B, H, dk, dk), F32),
        ],
        scratch_shapes=[
            pltpu.VMEM((n_chains, dk, dk), F32),
            pltpu.VMEM((2, n_chains, group, 2 * C, dk), BF16),
            pltpu.VMEM((2, n_chains, group, C, dk), F32),
            pltpu.VMEM((2, n_chains, group, C, C), BF16),
            pltpu.VMEM((2, n_chains, group, C, dk), BF16),
            pltpu.VMEM((2, n_chains, group, 8, dk), F32),
            pltpu.VMEM((Hb, L, dk), F32),
        ],
        compiler_params=_params(2),
    )(q, k, v, z, gcol, grow, s0f, s0b, norm_g.reshape(1, dk))


def _gdn_mixer(x_tok, B, mod, mod_row, w_qkv, w_z, w_ab, conv_w, a_log, dt_bias, norm_g, s0f, s0b, tile):
    n_tok = x_tok.shape[0]
    L = n_tok // B
    H, C = DN_HEADS, DN_CHUNK
    q, k, v, z, grows, gcol = _gdn_proj(x_tok, B, mod, mod_row, w_qkv, w_z, w_ab, conv_w, a_log, dt_bias, tile)
    grow = grows.reshape(B, 2, 2, H, L).transpose(0, 3, 1, 2, 4).reshape(B, H, 4, L // C, C)
    o, s_f, s_b = _gdn_core(q, k, v, z, gcol, grow, s0f, s0b, norm_g)
    return o.reshape(n_tok, -1), s_f, s_b


def kernel(x, c, ctx, c_ctx, w_mod, b_mod, ln_g, ln_b, pool_w, pool_scale, dn_w_in, dn_conv, dn_a_log, dn_dt_bias, dn_norm, dn_w_out, w_router, b_router, w_gate, w_up, w_down):
    B, L, D = x.shape
    Lc = ctx.shape[1]
    E = w_router.shape[1]
    F = w_gate.shape[-1]
    H = DN_HEADS
    dk = dn_norm.shape[-1]
    hk = H * dk
    ctx_row = B

    n_rows = -(-(B + 1) // 8) * 8
    cvecs = jnp.concatenate([c, c_ctx[None, :], jnp.zeros((n_rows - B - 1, D), F32)], axis=0)
    mods = _modulation(cvecs, w_mod, b_mod)
    ln = jnp.stack([ln_g, ln_b], axis=2).reshape(ln_g.shape[0], 4, D)
    w_router_hi = w_router.T.astype(BF16)
    w_router_t = jnp.stack([w_router_hi, (w_router.T - w_router_hi.astype(F32)).astype(BF16)])
    tile = 512
    x_row = lambda t: t // (L // tile)
    c_row = lambda t: ctx_row

    mod = mods[0]
    w_gu = jnp.concatenate([w_gate[0], w_up[0]], axis=-1).astype(BF16)
    w_d = w_down[0].reshape(E * F, D).astype(BF16)
    streams = []
    for tok, rows, row_of_batch, row_of_tile in ((x, L // GRID_W, lambda b: b, x_row), (ctx, None, lambda b: ctx_row, c_row)):
        y = _pool_mixer(tok, mod, row_of_batch, pool_w[0], pool_scale[0], rows)
        streams.append(_sublayers(tok.reshape(-1, D), y.reshape(-1, D), None, mod, row_of_tile, ln[0], w_router_t,
                                  b_router, w_gu, w_d, tile))
    x_tok, ctx_tok = streams

    mod = mods[1]
    w_in = dn_w_in[0]
    w_qkv = w_in[:, :3 * hk].astype(BF16)
    w_z = w_in[:, 3 * hk:4 * hk].astype(BF16)
    w_ab = w_in[:, 4 * hk:]
    conv_w = jnp.concatenate([dn_conv[0], jnp.zeros((8 - dn_conv.shape[1], 3 * hk), F32)], axis=0)
    a_log, dt_bias, norm_g = dn_a_log[0], dn_dt_bias[0], dn_norm[0]
    zeros = jnp.zeros((B, H, dk, dk), F32)
    _, s_f, s_b = _gdn_mixer(ctx_tok, B, mod, c_row, w_qkv, w_z, w_ab, conv_w, a_log, dt_bias, norm_g, zeros, zeros,
                             min(tile, Lc))
    o, _, _ = _gdn_mixer(x_tok, B, mod, x_row, w_qkv, w_z, w_ab, conv_w, a_log, dt_bias, norm_g, s_f, s_b, tile)
    w_gu = jnp.concatenate([w_gate[1], w_up[1]], axis=-1).astype(BF16)
    w_d = w_down[1].reshape(E * F, D).astype(BF16)
    out = _sublayers(x_tok, o, dn_w_out[0].astype(BF16), mod, x_row, ln[1], w_router_t, b_router, w_gu, w_d, tile)
    return out.reshape(B, L, D)
```

```python
import functools

import numpy as np
import jax
import jax.numpy as jnp
from jax import lax
from jax.experimental import pallas as pl
from jax.experimental.pallas import tpu as pltpu

GRID_W = 64
POOL_WINDOWS = (2, 4, 8, 16)
DN_HEADS = 8
DN_CHUNK = 64
DN_CONV_TAPS = 5
N_GROUPS = 4
EXPERTS_PER_GROUP = 4
N_EXPERTS = N_GROUPS * EXPERTS_PER_GROUP
DEPTH = 2
DEEPNORM_ALPHA = (2.0 * DEPTH) ** 0.25
LN_EPS = 1e-5

V7X_LANES = 128
V7X_VMEM_LIMIT_BYTES = 56 * 1024 * 1024

GDN_GROUP_CHUNKS = 8
GDN_GROUPS_PER_ITERATION = 2


def _gdn_heads_per_step(seq_len):
    return DN_HEADS if seq_len <= GDN_GROUP_CHUNKS * DN_CHUNK else DN_HEADS // 2

F32 = jnp.float32
BF16 = jnp.bfloat16
HIGHEST = lax.Precision.HIGHEST


def _params(n_grid_dims):
    return pltpu.CompilerParams(
        dimension_semantics=("arbitrary",) * n_grid_dims,
        vmem_limit_bytes=V7X_VMEM_LIMIT_BYTES,
    )


def _resident(shape):
    zeros = (0,) * len(shape)
    return pl.BlockSpec(shape, lambda *_: zeros, pipeline_mode=pl.Buffered(1))


def _silu(x):
    return x * jax.nn.sigmoid(x)


def _layer_norm(v, g, b):
    mu = jnp.mean(v, axis=-1, keepdims=True)
    cen = v - mu
    var = jnp.mean(cen * cen, axis=-1, keepdims=True)
    return cen * lax.rsqrt(var + LN_EPS) * g + b


def _mod_kernel(c_ref, w_ref, b_ref, o_ref):
    s = _silu(c_ref[...])
    o_ref[0] = jnp.dot(s, w_ref[0], precision=HIGHEST, preferred_element_type=F32) + b_ref[0]


def _modulation(cvecs, w_mod, b_mod):
    R, D = cvecs.shape
    depth, _, six_d = w_mod.shape
    tn = 1536
    out = pl.pallas_call(
        _mod_kernel,
        grid=(depth, six_d // tn),
        in_specs=[
            pl.BlockSpec((R, D), lambda i, n: (0, 0)),
            pl.BlockSpec((1, D, tn), lambda i, n: (i, 0, n)),
            pl.BlockSpec((1, 1, tn), lambda i, n: (i, 0, n)),
        ],
        out_specs=pl.BlockSpec((1, R, tn), lambda i, n: (i, 0, n)),
        out_shape=jax.ShapeDtypeStruct((depth, R, six_d), F32),
        compiler_params=_params(2),
    )(cvecs, w_mod, b_mod.reshape(depth, 1, six_d))
    return out.reshape(depth, R, 6, D)


def _pool_kernel(x_ref, mod_ref, band_ref, icnt_ref, w_ref, ps_ref, o_ref, *, grid_rows):
    L, Cg = x_ref.shape[1], x_ref.shape[2]
    slab = band_ref.shape[1]
    g = pl.program_id(1)

    def body(k):
        h = x_ref[0] * (1.0 + mod_ref[0, 1:2, :]) + mod_ref[0, 0:1, :]
        if grid_rows is None:
            s = h
        else:
            r3 = h.reshape(grid_rows, L // grid_rows, Cg)
            zpad = jnp.zeros((k // 2,) + r3.shape[1:], F32)
            win = jnp.concatenate([zpad, r3, zpad], axis=0)
            span = 1
            while span < k:
                win = win[:-span] + win[span:]
                span *= 2
            s = win[:grid_rows].reshape(L, Cg)
        band = band_ref[0]
        w = w_ref[0].astype(BF16)
        for i in range(L // slab):
            sl = slice(i * slab, (i + 1) * slab)
            s_i = s[sl]
            hi = s_i.astype(BF16)
            lo = (s_i - hi.astype(F32)).astype(BF16)
            tot = jnp.dot(band, hi, preferred_element_type=F32) + jnp.dot(band, lo, preferred_element_type=F32)
            ic = icnt_ref[0, sl, :]
            mean = tot * jnp.concatenate([ic] * (Cg // V7X_LANES), axis=-1)
            p = mean - h[sl]
            y = jnp.dot(p.astype(BF16), w, preferred_element_type=F32)
            o_ref[0, sl, :] = y * ps_ref[...]

    if grid_rows is None:
        body(None)
    else:
        for gi, k in enumerate(POOL_WINDOWS):
            pl.when(g == gi)(functools.partial(body, k))


def _pool_tables(L, grid_rows, slab):
    n_g = len(POOL_WINDOWS)
    band = np.zeros((n_g, slab, slab), np.float32)
    icnt = np.zeros((n_g, L), np.float32)
    t = np.arange(L)
    for gi, k in enumerate(POOL_WINDOWS):
        if grid_rows is None:
            pos, n, blk = np.arange(slab), L, np.zeros(slab, np.int64)
            cnt = np.clip(t + k - k // 2, 0, L) - np.clip(t - k // 2, 0, L)
        else:
            w = L // grid_rows
            pos, n, blk = np.arange(slab) % w, w, np.arange(slab) // w
            row, col = t // w, t % w
            cnt_r = np.clip(row + k - k // 2, 0, grid_rows) - np.clip(row - k // 2, 0, grid_rows)
            cnt_c = np.clip(col + k - k // 2, 0, w) - np.clip(col - k // 2, 0, w)
            cnt = cnt_r * cnt_c
        lo = np.clip(pos - k // 2, 0, n)
        hi = np.clip(pos + k - k // 2, 0, n)
        inside = (pos[None, :] >= lo[:, None]) & (pos[None, :] < hi[:, None]) & (blk[None, :] == blk[:, None])
        band[gi] = inside.astype(np.float32)
        icnt[gi] = 1.0 / cnt
    icnt = np.broadcast_to(icnt[:, :, None], (n_g, L, V7X_LANES))
    return jnp.asarray(band, BF16), jnp.asarray(icnt, F32)


def _pool_mixer(x, mod, mod_row, pool_w, pool_scale, grid_rows):
    B, L, D = x.shape
    n_g = len(POOL_WINDOWS)
    Cg = D // n_g
    slab = 256
    assert L % slab == 0 and (grid_rows is None or (L == slab or slab % (L // grid_rows) == 0))
    if grid_rows is None:
        assert L == slab
    band, icnt = _pool_tables(L, grid_rows, slab)
    return pl.pallas_call(
        functools.partial(_pool_kernel, grid_rows=grid_rows),
        grid=(B, n_g),
        in_specs=[
            pl.BlockSpec((1, L, Cg), lambda b, g: (b, 0, g)),
            pl.BlockSpec((1, 6, Cg), lambda b, g: (mod_row(b), 0, g)),
            pl.BlockSpec((1, slab, slab), lambda b, g: (g, 0, 0)),
            pl.BlockSpec((1, L, V7X_LANES), lambda b, g: (g, 0, 0)),
            pl.BlockSpec((1, Cg, Cg), lambda b, g: (g, 0, 0)),
            pl.BlockSpec((1, Cg), lambda b, g: (0, g)),
        ],
        out_specs=pl.BlockSpec((1, L, Cg), lambda b, g: (b, 0, g)),
        out_shape=jax.ShapeDtypeStruct((B, L, D), F32),
        compiler_params=_params(2),
    )(x, mod, band, icnt, pool_w, pool_scale.reshape(1, D))


def _route(scores, bias):
    sel = [s + b for s, b in zip(scores, bias)]
    E = EXPERTS_PER_GROUP
    grp = []
    for gi in range(N_GROUPS):
        v = sel[gi * E:(gi + 1) * E]
        best = None
        for a in range(E):
            for c in range(a + 1, E):
                pair = v[a] + v[c]
                best = pair if best is None else jnp.maximum(best, pair)
        grp.append(best)
    g_idx = jnp.zeros_like(grp[0], dtype=jnp.int32)
    g_best = grp[0]
    for gi in range(1, N_GROUPS):
        better = grp[gi] > g_best
        g_idx = jnp.where(better, gi, g_idx)
        g_best = jnp.where(better, grp[gi], g_best)
    in_sel, in_score = [], []
    for l in range(E):
        vs, vc = sel[l], scores[l]
        for gi in range(1, N_GROUPS):
            vs = jnp.where(g_idx == gi, sel[gi * E + l], vs)
            vc = jnp.where(g_idx == gi, scores[gi * E + l], vc)
        in_sel.append(vs)
        in_score.append(vc)
    i1 = jnp.zeros_like(g_idx)
    m1 = in_sel[0]
    for l in range(1, E):
        better = in_sel[l] > m1
        i1 = jnp.where(better, l, i1)
        m1 = jnp.where(better, in_sel[l], m1)
    i2 = jnp.full_like(g_idx, -1)
    m2 = jnp.full_like(m1, -jnp.inf)
    for l in range(E):
        better = (i1 != l) & ((in_sel[l] > m2) | (i2 < 0))
        i2 = jnp.where(better, l, i2)
        m2 = jnp.where(better, in_sel[l], m2)
    s1 = in_score[0]
    s2 = in_score[0]
    for l in range(1, E):
        s1 = jnp.where(i1 == l, in_score[l], s1)
        s2 = jnp.where(i2 == l, in_score[l], s2)
    tot = s1 + s2
    w1, w2 = s1 / tot, s2 / tot
    e1 = g_idx * E + i1
    e2 = g_idx * E + i2
    return [jnp.where(e1 == e, w1, 0.0) + jnp.where(e2 == e, w2, 0.0) for e in range(N_EXPERTS)]


def _sublayers_kernel(*refs, has_proj):
    if has_proj:
        x_ref, y_ref, wo_ref, mod_ref, ln_ref, wr_ref, br_ref, wgu_ref, wd_ref, o_ref, hid_ref, x1_ref, hb_ref = refs
    else:
        x_ref, y_ref, mod_ref, ln_ref, wr_ref, br_ref, wgu_ref, wd_ref, o_ref, hid_ref, x1_ref, hb_ref = refs
    T = x_ref.shape[0]
    F = wgu_ref.shape[2] // 2
    halves = (slice(0, T // 2), slice(T // 2, T))
    nt = (((1,), (1,)), ((), ()))
    n_early = 2
    logits, gus = [], [[] for _ in range(n_early)]
    y_full = jnp.dot(y_ref[...], wo_ref[...], preferred_element_type=F32) if has_proj else None
    for rs in halves:
        y = y_full[rs] if has_proj else y_ref[rs, :]
        v = DEEPNORM_ALPHA * x_ref[rs, :] + mod_ref[0, 2:3, :] * y
        x1 = _layer_norm(v, ln_ref[0:1, :], ln_ref[1:2, :])
        x1_ref[rs, :] = x1
        h = x1 * (1.0 + mod_ref[0, 4:5, :]) + mod_ref[0, 3:4, :]
        hb = h.astype(BF16)
        hb_ref[rs, :] = hb
        h_lo = (h - hb.astype(F32)).astype(BF16)
        logits.append(lax.dot_general(wr_ref[0], hb, nt, preferred_element_type=F32)
                      + lax.dot_general(wr_ref[1], hb, nt, preferred_element_type=F32)
                      + lax.dot_general(wr_ref[0], h_lo, nt, preferred_element_type=F32))
        for e in range(n_early):
            gus[e].append(jnp.dot(hb, wgu_ref[e], preferred_element_type=F32))
    logits = jnp.concatenate(logits, axis=1)
    hb = hb_ref[...]
    x1 = x1_ref[...]
    scores = jax.nn.sigmoid(logits)
    bias = br_ref[...]
    rows = _route([scores[e:e + 1, :] for e in range(N_EXPERTS)],
                  [jnp.broadcast_to(bias[e:e + 1, :], (1, T)) for e in range(N_EXPERTS)])
    cmb = jnp.concatenate(rows + [jnp.zeros((V7X_LANES - N_EXPERTS, T), F32)], axis=0).T
    for e in range(N_EXPERTS):
        if e < n_early:
            parts = zip(halves, gus[e])
        else:
            parts = [(slice(0, T), jnp.dot(hb, wgu_ref[e], preferred_element_type=F32))]
        for rs, gu in parts:
            hid = _silu(gu[:, :F]) * gu[:, F:] * cmb[rs, e:e + 1]
            hid_ref[rs, e * F:(e + 1) * F] = hid.astype(BF16)
    for rs in (slice(0, T // 2), slice(T // 2, T)):
        y2 = jnp.dot(hid_ref[rs, :], wd_ref[...], preferred_element_type=F32)
        v2 = DEEPNORM_ALPHA * x1[rs] + mod_ref[0, 5:6, :] * y2
        o_ref[rs, :] = _layer_norm(v2, ln_ref[2:3, :], ln_ref[3:4, :])


def _sublayers(x, y, w_out, mod, mod_row, ln, w_router_t, b_router, w_gu, w_d, tile):
    n_tok, D = x.shape
    E, _, F2 = w_gu.shape
    has_proj = w_out is not None
    tok = pl.BlockSpec((tile, D), lambda t: (t, 0))
    in_specs = [tok, tok]
    args = [x, y]
    if has_proj:
        in_specs.append(_resident(w_out.shape))
        args.append(w_out)
    in_specs += [
        pl.BlockSpec((1, 6, D), lambda t: (mod_row(t), 0, 0)),
        _resident(ln.shape),
        _resident(w_router_t.shape),
        _resident((E, 1)),
        _resident(w_gu.shape),
        _resident(w_d.shape),
    ]
    args += [mod, ln, w_router_t, b_router.reshape(E, 1), w_gu, w_d]
    return pl.pallas_call(
        functools.partial(_sublayers_kernel, has_proj=has_proj),
        grid=(n_tok // tile,),
        in_specs=in_specs,
        out_specs=tok,
        out_shape=jax.ShapeDtypeStruct((n_tok, D), F32),
        scratch_shapes=[
            pltpu.VMEM((tile, E * F2 // 2), BF16),
            pltpu.VMEM((tile, D), F32),
            pltpu.VMEM((tile, D), BF16),
        ],
        compiler_params=_params(1),
    )(*args)


def _proj_kernel(x_ref, xp_ref, xn_ref, mod_ref, wqkv_ref, wz_ref, wab_ref, cw_ref, alog_ref, dtb_ref,
                 q_ref, k_ref, v_ref, z_ref, grow_ref, gcol_ref, *, tiles_per_seq):
    T = x_ref.shape[0]
    H, dk = q_ref.shape[1], q_ref.shape[3]
    hk = H * dk
    H2 = alog_ref.shape[0]
    C = DN_CHUNK
    halo = xp_ref.shape[0]
    n_taps = DN_CONV_TAPS
    scale = 1.0 + mod_ref[0, 1:2, :]
    shift = mod_ref[0, 0:1, :]
    h = x_ref[...] * scale + shift
    pos = pl.program_id(0) % tiles_per_seq
    prev_ok = (pos != 0).astype(F32)
    next_ok = (pos != tiles_per_seq - 1).astype(F32)
    h_ext = jnp.concatenate([(xp_ref[...] * scale + shift) * prev_ok, h, (xn_ref[...] * scale + shift) * next_ok],
                            axis=0)
    u = jnp.dot(h_ext.astype(BF16), wqkv_ref[...], preferred_element_type=F32)
    acc = None
    for j in range(n_taps):
        d = j - n_taps // 2
        shifted = u if d == 0 else pltpu.roll(u, (-d) % (T + 2 * halo), axis=0)
        term = shifted[halo:halo + T] * cw_ref[j:j + 1, :]
        acc = term if acc is None else acc + term
    act = _silu(acc)
    for hd in range(H):
        qh = act[:, hd * dk:(hd + 1) * dk]
        kh = act[:, hk + hd * dk:hk + (hd + 1) * dk]
        q_ref[0, hd] = (qh * (lax.rsqrt(jnp.sum(qh * qh, axis=-1, keepdims=True) + 1e-6) * dk ** -0.5)).astype(BF16)
        k_ref[0, hd] = (kh * lax.rsqrt(jnp.sum(kh * kh, axis=-1, keepdims=True) + 1e-6)).astype(BF16)
        v_ref[0, hd] = act[:, 2 * hk + hd * dk:2 * hk + (hd + 1) * dk].astype(BF16)
    h_hi = h.astype(BF16)
    z = jnp.dot(h_hi, wz_ref[...], preferred_element_type=F32)
    for hd in range(H):
        z_ref[0, hd] = z[:, hd * dk:(hd + 1) * dk].astype(BF16)
    h_lo = (h - h_hi.astype(F32)).astype(BF16)
    nt = (((1,), (1,)), ((), ()))
    ab = (lax.dot_general(wab_ref[0], h_hi, nt, preferred_element_type=F32)
          + lax.dot_general(wab_ref[1], h_hi, nt, preferred_element_type=F32)
          + lax.dot_general(wab_ref[0], h_lo, nt, preferred_element_type=F32))
    a = ab[:H2] + dtb_ref[...]
    softplus = jnp.maximum(a, 0.0) + jnp.log1p(jnp.exp(-jnp.abs(a)))
    g = -jnp.exp(alog_ref[...]) * softplus
    beta = jax.nn.sigmoid(ab[H2:])
    pos = lax.broadcasted_iota(jnp.int32, (H2, T), 1) & (C - 1)
    pre = suf = g
    s = 1
    while s < C:
        pre = pre + jnp.where(pos >= s, pltpu.roll(pre, s, axis=1), 0.0)
        suf = suf + jnp.where(pos < C - s, pltpu.roll(suf, T - s, axis=1), 0.0)
        s *= 2
    tot = pre + suf - g
    fwd_rows = lax.broadcasted_iota(jnp.int32, (H2, T), 0) < H2 // 2
    gc = jnp.where(fwd_rows, pre, suf)
    grow_ref[0] = jnp.concatenate([gc, beta], axis=0)
    n_hg, Hb = gcol_ref.shape[0], H // gcol_ref.shape[0]
    for hg in range(n_hg):
        picks = [t[d * H + hg * Hb:d * H + (hg + 1) * Hb] for t in (gc, beta, tot) for d in range(2)]
        picks.append(jnp.zeros((V7X_LANES - 6 * Hb, T), F32))
        gcol_ref[hg] = jnp.concatenate(picks, axis=0).T


def _gdn_proj(x, B, mod, mod_row, w_qkv, w_z, w_ab, conv_w, a_log, dt_bias, tile):
    n_tok, D = x.shape
    L = n_tok // B
    H = DN_HEADS
    dk = w_z.shape[1] // H
    H2 = a_log.size
    n_hg = H // _gdn_heads_per_step(L)
    w_ab_t = w_ab.T
    w_ab_hi = w_ab_t.astype(BF16)
    w_ab_split = jnp.stack([w_ab_hi, (w_ab_t - w_ab_hi.astype(F32)).astype(BF16)])
    halo = 8
    tps = L // tile
    n_halo_blocks = n_tok // halo
    head_major = pl.BlockSpec((1, H, tile, dk), lambda t: (t // tps, 0, t % tps, 0))
    qkvz = jax.ShapeDtypeStruct((B, H, L, dk), BF16)
    return pl.pallas_call(
        functools.partial(_proj_kernel, tiles_per_seq=tps),
        grid=(n_tok // tile,),
        in_specs=[
            pl.BlockSpec((tile, D), lambda t: (t, 0)),
            pl.BlockSpec((halo, D), lambda t: (jnp.maximum(t * (tile // halo) - 1, 0), 0)),
            pl.BlockSpec((halo, D), lambda t: (jnp.minimum((t + 1) * (tile // halo), n_halo_blocks - 1), 0)),
            pl.BlockSpec((1, 6, D), lambda t: (mod_row(t), 0, 0)),
            _resident(w_qkv.shape),
            _resident(w_z.shape),
            _resident(w_ab_split.shape),
            _resident(conv_w.shape),
            _resident((H2, 1)),
            _resident((H2, 1)),
        ],
        out_specs=[head_major, head_major, head_major, head_major,
                   pl.BlockSpec((1, 2 * H2, tile), lambda t: (t // tps, 0, t % tps)),
                   pl.BlockSpec((n_hg, tile, V7X_LANES), lambda t: (0, t, 0))],
        out_shape=[qkvz, qkvz, qkvz, qkvz, jax.ShapeDtypeStruct((B, 2 * H2, L), F32),
                   jax.ShapeDtypeStruct((n_hg, n_tok, V7X_LANES), F32)],
        compiler_params=_params(1),
    )(x, x, x, mod, w_qkv, w_z, w_ab_split, conv_w, a_log.reshape(H2, 1), dt_bias.reshape(H2, 1))


def _gdn_kernel(q_ref, k_ref, v_ref, z_ref, gcol_ref, grow_ref, s0f_ref, s0b_ref, ng_ref,
                o_ref, sf_ref, sb_ref,
                s_scr, wq_scr, u_scr, qk_scr, kd_scr, gl_scr, oacc_scr, *, group):
    Hb, L, dk = q_ref.shape[1], q_ref.shape[2], q_ref.shape[3]
    C = DN_CHUNK
    N = L // C
    n_groups = N // group
    gtok = group * C

    m = Hb * group
    for hh in range(Hb):
        s_scr[hh] = s0f_ref[0, hh]
        s_scr[Hb + hh] = s0b_ref[0, hh]
    oacc_scr[...] = jnp.zeros(oacc_scr.shape, F32)

    ii = lax.broadcasted_iota(jnp.int32, (C, C), 0)
    jj = lax.broadcasted_iota(jnp.int32, (C, C), 1)
    eye = (ii == jj).astype(F32)
    off_masks = []
    s = 1
    while s < C:
        same = (ii & -(2 * s)) == (jj & -(2 * s))
        lo = (same & ((ii & s) != 0) & ((jj & s) == 0)).astype(F32)
        up = (same & ((jj & s) != 0) & ((ii & s) == 0)).astype(F32)
        off_masks.append(jnp.stack([lo, up])[:, None])
        s *= 2

    def bmm(a, b):
        return jnp.einsum('nij,njd->nid', a, b, preferred_element_type=F32)

    def bmm_nt(a, b):
        return jnp.einsum('nid,njd->nij', a, b, preferred_element_type=F32)

    def heads(read):
        return jnp.concatenate([read(hh) for hh in range(Hb)], axis=0)

    def local(cgs, par):
        per_dir = []
        for d in range(2):
            tok = pl.ds(pl.multiple_of(cgs[d] * gtok, gtok), gtok)
            rows = pl.ds(pl.multiple_of(cgs[d] * group, group), group)
            q = heads(lambda hh: q_ref[0, hh, tok, :].reshape(group, C, dk))
            k = heads(lambda hh: k_ref[0, hh, tok, :].reshape(group, C, dk))
            v = heads(lambda hh: v_ref[0, hh, tok, :].reshape(group, C, dk))
            gcols = gcol_ref[0, tok, :]

            def col(kind):
                lane0 = kind * 2 * Hb + d * Hb
                return heads(lambda hh: gcols[:, lane0 + hh:lane0 + hh + 1].reshape(group, C, 1))

            gc, beta, gt = col(0), col(1), col(2)
            gc_row = heads(lambda hh: grow_ref[0, hh, d, rows, :])[:, None, :]
            beta_row = heads(lambda hh: grow_ref[0, hh, 2 + d, rows, :])[:, None, :]
            strict = (ii > jj) if d == 0 else (ii < jj)
            decay = jnp.exp(jnp.where(strict[None], gc - gc_row, -jnp.inf))
            a = bmm_nt(k, k) * beta * decay
            chains = slice(d * Hb, (d + 1) * Hb)
            qk_scr[par, chains] = (bmm_nt(q, k) * (decay + eye[None])).astype(BF16).reshape(Hb, group, C, C)
            q_dec = (q.astype(F32) * jnp.exp(gc)).astype(BF16)
            kd_scr[par, chains] = (k.astype(F32) * jnp.exp(gt - gc)).astype(BF16).reshape(Hb, group, C, dk)
            gl_scr[par, chains] = jnp.broadcast_to(jnp.exp(gt[:, 0:1, :]), (m, 8, dk)).reshape(Hb, group, 8, dk)
            per_dir.append((a, k, v, q_dec, gc_row, beta_row))
            yield
        a4 = jnp.stack([per_dir[0][0], per_dir[1][0]])
        t_inv = eye[None] - (a4 * off_masks[0]).reshape(2 * m, C, C)
        for mask in off_masks[1:]:
            t_b = t_inv.astype(BF16)
            a_off = (a4 * mask).reshape(2 * m, C, C).astype(BF16)
            t_inv = t_inv - bmm(t_b, bmm(a_off, t_b).astype(BF16))
            yield
        for d in range(2):
            _, k, v, q_dec, gc_row, beta_row = per_dir[d]
            chains = slice(d * Hb, (d + 1) * Hb)
            t_beta = t_inv[d * m:(d + 1) * m] * beta_row
            u_scr[par, chains] = bmm(t_beta.astype(BF16), v).reshape(Hb, group, C, dk)
            w = bmm((t_beta * jnp.exp(gc_row)).astype(BF16), k)
            wq_scr[par, chains] = jnp.concatenate([w.astype(BF16), q_dec], axis=1).reshape(Hb, group, 2 * C, dk)
            yield

    def step(cgs, par, i):
        slots = [(d, hh) for d in range(2) for hh in range(Hb)]
        ns = [i if d == 0 else group - 1 - i for d, _ in slots]
        states = [s_scr[c] for c in range(2 * Hb)]
        r = [jnp.dot(wq_scr[par, c, ns[c]], states[c].astype(BF16), preferred_element_type=F32)
             for c in range(2 * Hb)]
        v_new = [(u_scr[par, c, ns[c]] - r[c][:C]).astype(BF16) for c in range(2 * Hb)]
        o = [jnp.dot(qk_scr[par, c, ns[c]], v_new[c], preferred_element_type=F32) for c in range(2 * Hb)]
        upd = [lax.dot_general(kd_scr[par, c, ns[c]], v_new[c], (((0,), (0,)), ((), ())),
                               preferred_element_type=F32) for c in range(2 * Hb)]
        for c, (d, hh) in enumerate(slots):
            s_scr[c] = states[c] * gl_scr[par, c, ns[c]][0:1, :] + upd[c]
            rows = pl.ds(pl.multiple_of((cgs[d] * group + ns[c]) * C, C), C)
            oacc_scr[hh, rows, :] += r[c][C:] + o[c]

    span = min(n_groups, GDN_GROUPS_PER_ITERATION)
    group_of = lambda g: (g, n_groups - 1 - g)

    def run_span(it, _):
        g0 = it * span
        for _ in local(group_of(g0), 0):
            pass
        for j in range(span):
            ahead = local(group_of(g0 + j + 1), (j + 1) % 2) if j + 1 < span else iter(())
            for i in range(group):
                step(group_of(g0 + j), j % 2, i)
                next(ahead, None)
            for _ in ahead:
                pass
        return 0

    lax.fori_loop(0, n_groups // span, run_span, 0)

    for hh in range(Hb):
        sf_ref[0, hh] = s_scr[hh]
        sb_ref[0, hh] = s_scr[Hb + hh]
        o = oacc_scr[hh]
        o = o * lax.rsqrt(jnp.mean(o * o, axis=-1, keepdims=True) + 1e-6) * ng_ref[...]
        o_ref[0, :, hh * dk:(hh + 1) * dk] = (o * _silu(z_ref[0, hh].astype(F32))).astype(BF16)


def _gdn_core(q, k, v, z, gcol, grow, s0f, s0b, norm_g):
    B, H, L, dk = q.shape
    C = DN_CHUNK
    N = L // C
    group = min(N, GDN_GROUP_CHUNKS)
    Hb = H // gcol.shape[0]
    n_chains = 2 * Hb
    seq = pl.BlockSpec((1, Hb, L, dk), lambda b, h: (b, h, 0, 0))
    st = pl.BlockSpec((1, Hb, dk, dk), lambda b, h: (b, h, 0, 0))
    return pl.pallas_call(
        functools.partial(_gdn_kernel, group=group),
        grid=(B, H // Hb),
        in_specs=[
            seq, seq, seq, seq,
            pl.BlockSpec((1, L, V7X_LANES), lambda b, h: (h, b, 0)),
            pl.BlockSpec((1, Hb, 4, N, C), lambda b, h: (b, h, 0, 0, 0)),
            st, st,
            pl.BlockSpec((1, dk), lambda b, h: (0, 0)),
        ],
        out_specs=[pl.BlockSpec((1, L, Hb * dk), lambda b, h: (b, 0, h)), st, st],
        out_shape=[
            jax.ShapeDtypeStruct((B, L, H * dk), BF16),
            jax.ShapeDtypeStruct((B, H, dk, dk), F32),
            jax.ShapeDtypeStruct((B, H, dk, dk), F32),
        ],
        scratch_shapes=[
            pltpu.VMEM((n_chains, dk, dk), F32),
            pltpu.VMEM((2, n_chains, group, 2 * C, dk), BF16),
            pltpu.VMEM((2, n_chains, group, C, dk), F32),
            pltpu.VMEM((2, n_chains, group, C, C), BF16),
            pltpu.VMEM((2, n_chains, group, C, dk), BF16),
            pltpu.VMEM((2, n_chains, group, 8, dk), F32),
            pltpu.VMEM((Hb, L, dk), F32),
        ],
        compiler_params=_params(2),
    )(q, k, v, z, gcol, grow, s0f, s0b, norm_g.reshape(1, dk))


def _gdn_mixer(x_tok, B, mod, mod_row, w_qkv, w_z, w_ab, conv_w, a_log, dt_bias, norm_g, s0f, s0b, tile):
    n_tok = x_tok.shape[0]
    L = n_tok // B
    H, C = DN_HEADS, DN_CHUNK
    q, k, v, z, grows, gcol = _gdn_proj(x_tok, B, mod, mod_row, w_qkv, w_z, w_ab, conv_w, a_log, dt_bias, tile)
    grow = grows.reshape(B, 2, 2, H, L).transpose(0, 3, 1, 2, 4).reshape(B, H, 4, L // C, C)
    o, s_f, s_b = _gdn_core(q, k, v, z, gcol, grow, s0f, s0b, norm_g)
    return o.reshape(n_tok, -1), s_f, s_b


def kernel(x, c, ctx, c_ctx, w_mod, b_mod, ln_g, ln_b, pool_w, pool_scale, dn_w_in, dn_conv, dn_a_log, dn_dt_bias, dn_norm, dn_w_out, w_router, b_router, w_gate, w_up, w_down):
    B, L, D = x.shape
    Lc = ctx.shape[1]
    E = w_router.shape[1]
    F = w_gate.shape[-1]
    H = DN_HEADS
    dk = dn_norm.shape[-1]
    hk = H * dk
    ctx_row = B

    n_rows = -(-(B + 1) // 8) * 8
    cvecs = jnp.concatenate([c, c_ctx[None, :], jnp.zeros((n_rows - B - 1, D), F32)], axis=0)
    mods = _modulation(cvecs, w_mod, b_mod)
    ln = jnp.stack([ln_g, ln_b], axis=2).reshape(ln_g.shape[0], 4, D)
    w_router_hi = w_router.T.astype(BF16)
    w_router_t = jnp.stack([w_router_hi, (w_router.T - w_router_hi.astype(F32)).astype(BF16)])
    tile = 512
    x_row = lambda t: t // (L // tile)
    c_row = lambda t: ctx_row

    mod = mods[0]
    w_gu = jnp.concatenate([w_gate[0], w_up[0]], axis=-1).astype(BF16)
    w_d = w_down[0].reshape(E * F, D).astype(BF16)
    streams = []
    for tok, rows, row_of_batch, row_of_tile in ((x, L // GRID_W, lambda b: b, x_row), (ctx, None, lambda b: ctx_row, c_row)):
        y = _pool_mixer(tok, mod, row_of_batch, pool_w[0], pool_scale[0], rows)
        streams.append(_sublayers(tok.reshape(-1, D), y.reshape(-1, D), None, mod, row_of_tile, ln[0], w_router_t,
                                  b_router, w_gu, w_d, tile))
    x_tok, ctx_tok = streams

    mod = mods[1]
    w_in = dn_w_in[0]
    w_qkv = w_in[:, :3 * hk].astype(BF16)
    w_z = w_in[:, 3 * hk:4 * hk].astype(BF16)
    w_ab = w_in[:, 4 * hk:]
    conv_w = jnp.concatenate([dn_conv[0], jnp.zeros((8 - dn_conv.shape[1], 3 * hk), F32)], axis=0)
    a_log, dt_bias, norm_g = dn_a_log[0], dn_dt_bias[0], dn_norm[0]
    zeros = jnp.zeros((B, H, dk, dk), F32)
    _, s_f, s_b = _gdn_mixer(ctx_tok, B, mod, c_row, w_qkv, w_z, w_ab, conv_w, a_log, dt_bias, norm_g, zeros, zeros,
                             min(tile, Lc))
    o, _, _ = _gdn_mixer(x_tok, B, mod, x_row, w_qkv, w_z, w_ab, conv_w, a_log, dt_bias, norm_g, s_f, s_b, tile)
    w_gu = jnp.concatenate([w_gate[1], w_up[1]], axis=-1).astype(BF16)
    w_d = w_down[1].reshape(E * F, D).astype(BF16)
    out = _sublayers(x_tok, o, dn_w_out[0].astype(BF16), mod, x_row, ln[1], w_router_t, b_router, w_gu, w_d, tile)
    return out.reshape(B, L, D)
```

```python
import functools

import numpy as np
import jax
import jax.numpy as jnp
from jax import lax
from jax.experimental import pallas as pl
from jax.experimental.pallas import tpu as pltpu

GRID_W = 64
POOL_WINDOWS = (2, 4, 8, 16)
DN_HEADS = 8
DN_CHUNK = 64
DN_CONV_TAPS = 5
N_GROUPS = 4
EXPERTS_PER_GROUP = 4
N_EXPERTS = N_GROUPS * EXPERTS_PER_GROUP
DEPTH = 2
DEEPNORM_ALPHA = (2.0 * DEPTH) ** 0.25
LN_EPS = 1e-5

V7X_LANES = 128
V7X_SUBLANES = 8
V7X_VMEM_LIMIT_BYTES = 56 * 1024 * 1024

TOKEN_TILE = 512
MOD_COLUMN_TILE = 1536
POOL_SLAB = 256
GDN_GROUP_CHUNKS = 8
GDN_GROUPS_PER_ITERATION = 2

F32 = jnp.float32
BF16 = jnp.bfloat16
HIGHEST = lax.Precision.HIGHEST


def _gdn_heads_per_step(seq_len):
    return DN_HEADS if seq_len <= GDN_GROUP_CHUNKS * DN_CHUNK else DN_HEADS // 2


def _params(n_grid_dims):
    return pltpu.CompilerParams(
        dimension_semantics=("arbitrary",) * n_grid_dims,
        vmem_limit_bytes=V7X_VMEM_LIMIT_BYTES,
    )


def _resident(shape):
    zeros = (0,) * len(shape)
    return pl.BlockSpec(shape, lambda *_: zeros, pipeline_mode=pl.Buffered(1))


def _silu(x):
    return x * jax.nn.sigmoid(x)


def _layer_norm(v, g, b):
    mu = jnp.mean(v, axis=-1, keepdims=True)
    cen = v - mu
    var = jnp.mean(cen * cen, axis=-1, keepdims=True)
    return cen * lax.rsqrt(var + LN_EPS) * g + b


def _mod_kernel(c_ref, w_ref, b_ref, o_ref):
    s = _silu(c_ref[...])
    o_ref[0] = jnp.dot(s, w_ref[0], precision=HIGHEST, preferred_element_type=F32) + b_ref[0]


def _modulation(cvecs, w_mod, b_mod):
    R, D = cvecs.shape
    depth, _, six_d = w_mod.shape
    tn = MOD_COLUMN_TILE
    out = pl.pallas_call(
        _mod_kernel,
        grid=(depth, six_d // tn),
        in_specs=[
            pl.BlockSpec((R, D), lambda i, n: (0, 0)),
            pl.BlockSpec((1, D, tn), lambda i, n: (i, 0, n)),
            pl.BlockSpec((1, 1, tn), lambda i, n: (i, 0, n)),
        ],
        out_specs=pl.BlockSpec((1, R, tn), lambda i, n: (i, 0, n)),
        out_shape=jax.ShapeDtypeStruct((depth, R, six_d), F32),
        compiler_params=_params(2),
    )(cvecs, w_mod, b_mod.reshape(depth, 1, six_d))
    return out.reshape(depth, R, 6, D)


def _pool_kernel(x_ref, mod_ref, band_ref, icnt_ref, w_ref, ps_ref, o_ref, *, grid_rows):
    L, Cg = x_ref.shape[1], x_ref.shape[2]
    slab = band_ref.shape[1]
    g = pl.program_id(1)

    def body(k):
        h = x_ref[0] * (1.0 + mod_ref[0, 1:2, :]) + mod_ref[0, 0:1, :]
        if grid_rows is None:
            s = h
        else:
            r3 = h.reshape(grid_rows, L // grid_rows, Cg)
            zpad = jnp.zeros((k // 2,) + r3.shape[1:], F32)
            win = jnp.concatenate([zpad, r3, zpad], axis=0)
            span = 1
            while span < k:
                win = win[:-span] + win[span:]
                span *= 2
            s = win[:grid_rows].reshape(L, Cg)
        band = band_ref[0]
        w = w_ref[0].astype(BF16)
        for i in range(L // slab):
            sl = slice(i * slab, (i + 1) * slab)
            s_i = s[sl]
            hi = s_i.astype(BF16)
            lo = (s_i - hi.astype(F32)).astype(BF16)
            tot = jnp.dot(band, hi, preferred_element_type=F32) + jnp.dot(band, lo, preferred_element_type=F32)
            ic = icnt_ref[0, sl, :]
            mean = tot * jnp.concatenate([ic] * (Cg // V7X_LANES), axis=-1)
            p = mean - h[sl]
            y = jnp.dot(p.astype(BF16), w, preferred_element_type=F32)
            o_ref[0, sl, :] = y * ps_ref[...]

    if grid_rows is None:
        body(None)
    else:
        for gi, k in enumerate(POOL_WINDOWS):
            pl.when(g == gi)(functools.partial(body, k))


def _pool_tables(L, grid_rows, slab):
    n_g = len(POOL_WINDOWS)
    band = np.zeros((n_g, slab, slab), np.float32)
    icnt = np.zeros((n_g, L), np.float32)
    t = np.arange(L)
    for gi, k in enumerate(POOL_WINDOWS):
        if grid_rows is None:
            pos, n, blk = np.arange(slab), L, np.zeros(slab, np.int64)
            cnt = np.clip(t + k - k // 2, 0, L) - np.clip(t - k // 2, 0, L)
        else:
            w = L // grid_rows
            pos, n, blk = np.arange(slab) % w, w, np.arange(slab) // w
            row, col = t // w, t % w
            cnt_r = np.clip(row + k - k // 2, 0, grid_rows) - np.clip(row - k // 2, 0, grid_rows)
            cnt_c = np.clip(col + k - k // 2, 0, w) - np.clip(col - k // 2, 0, w)
            cnt = cnt_r * cnt_c
        lo = np.clip(pos - k // 2, 0, n)
        hi = np.clip(pos + k - k // 2, 0, n)
        inside = (pos[None, :] >= lo[:, None]) & (pos[None, :] < hi[:, None]) & (blk[None, :] == blk[:, None])
        band[gi] = inside.astype(np.float32)
        icnt[gi] = 1.0 / cnt
    icnt = np.broadcast_to(icnt[:, :, None], (n_g, L, V7X_LANES))
    return jnp.asarray(band, BF16), jnp.asarray(icnt, F32)


def _pool_mixer(x, mod, mod_row, pool_w, pool_scale, grid_rows):
    B, L, D = x.shape
    n_g = len(POOL_WINDOWS)
    Cg = D // n_g
    slab = POOL_SLAB
    assert L == slab if grid_rows is None else (L % slab == 0 and slab % (L // grid_rows) == 0)
    band, icnt = _pool_tables(L, grid_rows, slab)
    return pl.pallas_call(
        functools.partial(_pool_kernel, grid_rows=grid_rows),
        grid=(B, n_g),
        in_specs=[
            pl.BlockSpec((1, L, Cg), lambda b, g: (b, 0, g)),
            pl.BlockSpec((1, 6, Cg), lambda b, g: (mod_row(b), 0, g)),
            pl.BlockSpec((1, slab, slab), lambda b, g: (g, 0, 0)),
            pl.BlockSpec((1, L, V7X_LANES), lambda b, g: (g, 0, 0)),
            pl.BlockSpec((1, Cg, Cg), lambda b, g: (g, 0, 0)),
            pl.BlockSpec((1, Cg), lambda b, g: (0, g)),
        ],
        out_specs=pl.BlockSpec((1, L, Cg), lambda b, g: (b, 0, g)),
        out_shape=jax.ShapeDtypeStruct((B, L, D), F32),
        compiler_params=_params(2),
    )(x, mod, band, icnt, pool_w, pool_scale.reshape(1, D))


def _route(scores, bias):
    sel = [s + b for s, b in zip(scores, bias)]
    E = EXPERTS_PER_GROUP
    grp = []
    for gi in range(N_GROUPS):
        v = sel[gi * E:(gi + 1) * E]
        best = None
        for a in range(E):
            for c in range(a + 1, E):
                pair = v[a] + v[c]
                best = pair if best is None else jnp.maximum(best, pair)
        grp.append(best)
    g_idx = jnp.zeros_like(grp[0], dtype=jnp.int32)
    g_best = grp[0]
    for gi in range(1, N_GROUPS):
        better = grp[gi] > g_best
        g_idx = jnp.where(better, gi, g_idx)
        g_best = jnp.where(better, grp[gi], g_best)
    in_sel, in_score = [], []
    for l in range(E):
        vs, vc = sel[l], scores[l]
        for gi in range(1, N_GROUPS):
            vs = jnp.where(g_idx == gi, sel[gi * E + l], vs)
            vc = jnp.where(g_idx == gi, scores[gi * E + l], vc)
        in_sel.append(vs)
        in_score.append(vc)
    i1 = jnp.zeros_like(g_idx)
    m1 = in_sel[0]
    for l in range(1, E):
        better = in_sel[l] > m1
        i1 = jnp.where(better, l, i1)
        m1 = jnp.where(better, in_sel[l], m1)
    i2 = jnp.full_like(g_idx, -1)
    m2 = jnp.full_like(m1, -jnp.inf)
    for l in range(E):
        better = (i1 != l) & ((in_sel[l] > m2) | (i2 < 0))
        i2 = jnp.where(better, l, i2)
        m2 = jnp.where(better, in_sel[l], m2)
    s1 = in_score[0]
    s2 = in_score[0]
    for l in range(1, E):
        s1 = jnp.where(i1 == l, in_score[l], s1)
        s2 = jnp.where(i2 == l, in_score[l], s2)
    tot = s1 + s2
    w1, w2 = s1 / tot, s2 / tot
    e1 = g_idx * E + i1
    e2 = g_idx * E + i2
    return [jnp.where(e1 == e, w1, 0.0) + jnp.where(e2 == e, w2, 0.0) for e in range(N_EXPERTS)]


def _sublayers_kernel(*refs, has_proj):
    if has_proj:
        x_ref, y_ref, wo_ref, mod_ref, ln_ref, wr_ref, br_ref, wgu_ref, wd_ref, o_ref, hid_ref, x1_ref, hb_ref = refs
    else:
        x_ref, y_ref, mod_ref, ln_ref, wr_ref, br_ref, wgu_ref, wd_ref, o_ref, hid_ref, x1_ref, hb_ref = refs
    T = x_ref.shape[0]
    F = wgu_ref.shape[2] // 2
    halves = (slice(0, T // 2), slice(T // 2, T))
    nt = (((1,), (1,)), ((), ()))
    n_early = 2
    logits, gus = [], [[] for _ in range(n_early)]
    y_full = jnp.dot(y_ref[...], wo_ref[...], preferred_element_type=F32) if has_proj else None
    for rs in halves:
        y = y_full[rs] if has_proj else y_ref[rs, :]
        v = DEEPNORM_ALPHA * x_ref[rs, :] + mod_ref[0, 2:3, :] * y
        x1 = _layer_norm(v, ln_ref[0:1, :], ln_ref[1:2, :])
        x1_ref[rs, :] = x1
        h = x1 * (1.0 + mod_ref[0, 4:5, :]) + mod_ref[0, 3:4, :]
        hb = h.astype(BF16)
        hb_ref[rs, :] = hb
        h_lo = (h - hb.astype(F32)).astype(BF16)
        logits.append(lax.dot_general(wr_ref[0], hb, nt, preferred_element_type=F32)
                      + lax.dot_general(wr_ref[1], hb, nt, preferred_element_type=F32)
                      + lax.dot_general(wr_ref[0], h_lo, nt, preferred_element_type=F32))
        for e in range(n_early):
            gus[e].append(jnp.dot(hb, wgu_ref[e], preferred_element_type=F32))
    logits = jnp.concatenate(logits, axis=1)
    hb = hb_ref[...]
    x1 = x1_ref[...]
    scores = jax.nn.sigmoid(logits)
    bias = br_ref[...]
    rows = _route([scores[e:e + 1, :] for e in range(N_EXPERTS)],
                  [jnp.broadcast_to(bias[e:e + 1, :], (1, T)) for e in range(N_EXPERTS)])
    cmb = jnp.concatenate(rows + [jnp.zeros((V7X_LANES - N_EXPERTS, T), F32)], axis=0).T
    for e in range(N_EXPERTS):
        if e < n_early:
            parts = zip(halves, gus[e])
        else:
            parts = [(slice(0, T), jnp.dot(hb, wgu_ref[e], preferred_element_type=F32))]
        for rs, gu in parts:
            hid = _silu(gu[:, :F]) * gu[:, F:] * cmb[rs, e:e + 1]
            hid_ref[rs, e * F:(e + 1) * F] = hid.astype(BF16)
    for rs in (slice(0, T // 2), slice(T // 2, T)):
        y2 = jnp.dot(hid_ref[rs, :], wd_ref[...], preferred_element_type=F32)
        v2 = DEEPNORM_ALPHA * x1[rs] + mod_ref[0, 5:6, :] * y2
        o_ref[rs, :] = _layer_norm(v2, ln_ref[2:3, :], ln_ref[3:4, :])


def _sublayers(x, y, w_out, mod, mod_row, ln, w_router_t, b_router, w_gu, w_d, tile):
    n_tok, D = x.shape
    E, _, F2 = w_gu.shape
    has_proj = w_out is not None
    tok = pl.BlockSpec((tile, D), lambda t: (t, 0))
    in_specs = [tok, tok]
    args = [x, y]
    if has_proj:
        in_specs.append(_resident(w_out.shape))
        args.append(w_out)
    in_specs += [
        pl.BlockSpec((1, 6, D), lambda t: (mod_row(t), 0, 0)),
        _resident(ln.shape),
        _resident(w_router_t.shape),
        _resident((E, 1)),
        _resident(w_gu.shape),
        _resident(w_d.shape),
    ]
    args += [mod, ln, w_router_t, b_router.reshape(E, 1), w_gu, w_d]
    return pl.pallas_call(
        functools.partial(_sublayers_kernel, has_proj=has_proj),
        grid=(n_tok // tile,),
        in_specs=in_specs,
        out_specs=tok,
        out_shape=jax.ShapeDtypeStruct((n_tok, D), F32),
        scratch_shapes=[
            pltpu.VMEM((tile, E * F2 // 2), BF16),
            pltpu.VMEM((tile, D), F32),
            pltpu.VMEM((tile, D), BF16),
        ],
        compiler_params=_params(1),
    )(*args)


def _proj_kernel(x_ref, xp_ref, xn_ref, mod_ref, wqkv_ref, wz_ref, wab_ref, cw_ref, alog_ref, dtb_ref,
                 q_ref, k_ref, v_ref, z_ref, grow_ref, gcol_ref, *, tiles_per_seq):
    T = x_ref.shape[0]
    H, dk = q_ref.shape[1], q_ref.shape[3]
    hk = H * dk
    H2 = alog_ref.shape[0]
    C = DN_CHUNK
    halo = xp_ref.shape[0]
    n_taps = DN_CONV_TAPS
    scale = 1.0 + mod_ref[0, 1:2, :]
    shift = mod_ref[0, 0:1, :]
    h = x_ref[...] * scale + shift
    pos = pl.program_id(0) % tiles_per_seq
    prev_ok = (pos != 0).astype(F32)
    next_ok = (pos != tiles_per_seq - 1).astype(F32)
    h_ext = jnp.concatenate([(xp_ref[...] * scale + shift) * prev_ok, h, (xn_ref[...] * scale + shift) * next_ok],
                            axis=0)
    u = jnp.dot(h_ext.astype(BF16), wqkv_ref[...], preferred_element_type=F32)
    acc = None
    for j in range(n_taps):
        d = j - n_taps // 2
        shifted = u if d == 0 else pltpu.roll(u, (-d) % (T + 2 * halo), axis=0)
        term = shifted[halo:halo + T] * cw_ref[j:j + 1, :]
        acc = term if acc is None else acc + term
    act = _silu(acc)
    for hd in range(H):
        qh = act[:, hd * dk:(hd + 1) * dk]
        kh = act[:, hk + hd * dk:hk + (hd + 1) * dk]
        q_ref[0, hd] = (qh * (lax.rsqrt(jnp.sum(qh * qh, axis=-1, keepdims=True) + 1e-6) * dk ** -0.5)).astype(BF16)
        k_ref[0, hd] = (kh * lax.rsqrt(jnp.sum(kh * kh, axis=-1, keepdims=True) + 1e-6)).astype(BF16)
        v_ref[0, hd] = act[:, 2 * hk + hd * dk:2 * hk + (hd + 1) * dk].astype(BF16)
    h_hi = h.astype(BF16)
    z = jnp.dot(h_hi, wz_ref[...], preferred_element_type=F32)
    for hd in range(H):
        z_ref[0, hd] = z[:, hd * dk:(hd + 1) * dk].astype(BF16)
    h_lo = (h - h_hi.astype(F32)).astype(BF16)
    nt = (((1,), (1,)), ((), ()))
    ab = (lax.dot_general(wab_ref[0], h_hi, nt, preferred_element_type=F32)
          + lax.dot_general(wab_ref[1], h_hi, nt, preferred_element_type=F32)
          + lax.dot_general(wab_ref[0], h_lo, nt, preferred_element_type=F32))
    a = ab[:H2] + dtb_ref[...]
    softplus = jnp.maximum(a, 0.0) + jnp.log1p(jnp.exp(-jnp.abs(a)))
    g = -jnp.exp(alog_ref[...]) * softplus
    beta = jax.nn.sigmoid(ab[H2:])
    pos = lax.broadcasted_iota(jnp.int32, (H2, T), 1) & (C - 1)
    pre = suf = g
    s = 1
    while s < C:
        pre = pre + jnp.where(pos >= s, pltpu.roll(pre, s, axis=1), 0.0)
        suf = suf + jnp.where(pos < C - s, pltpu.roll(suf, T - s, axis=1), 0.0)
        s *= 2
    tot = pre + suf - g
    fwd_rows = lax.broadcasted_iota(jnp.int32, (H2, T), 0) < H2 // 2
    gc = jnp.where(fwd_rows, pre, suf)
    grow_ref[0] = jnp.concatenate([gc, beta], axis=0)
    n_hg, Hb = gcol_ref.shape[0], H // gcol_ref.shape[0]
    for hg in range(n_hg):
        picks = [t[d * H + hg * Hb:d * H + (hg + 1) * Hb] for t in (gc, beta, tot) for d in range(2)]
        picks.append(jnp.zeros((V7X_LANES - 6 * Hb, T), F32))
        gcol_ref[hg] = jnp.concatenate(picks, axis=0).T


def _gdn_proj(x, B, mod, mod_row, w_qkv, w_z, w_ab, conv_w, a_log, dt_bias, tile):
    n_tok, D = x.shape
    L = n_tok // B
    H = DN_HEADS
    dk = w_z.shape[1] // H
    H2 = a_log.size
    n_hg = H // _gdn_heads_per_step(L)
    w_ab_t = w_ab.T
    w_ab_hi = w_ab_t.astype(BF16)
    w_ab_split = jnp.stack([w_ab_hi, (w_ab_t - w_ab_hi.astype(F32)).astype(BF16)])
    halo = V7X_SUBLANES
    tps = L // tile
    n_halo_blocks = n_tok // halo
    head_major = pl.BlockSpec((1, H, tile, dk), lambda t: (t // tps, 0, t % tps, 0))
    qkvz = jax.ShapeDtypeStruct((B, H, L, dk), BF16)
    return pl.pallas_call(
        functools.partial(_proj_kernel, tiles_per_seq=tps),
        grid=(n_tok // tile,),
        in_specs=[
            pl.BlockSpec((tile, D), lambda t: (t, 0)),
            pl.BlockSpec((halo, D), lambda t: (jnp.maximum(t * (tile // halo) - 1, 0), 0)),
            pl.BlockSpec((halo, D), lambda t: (jnp.minimum((t + 1) * (tile // halo), n_halo_blocks - 1), 0)),
            pl.BlockSpec((1, 6, D), lambda t: (mod_row(t), 0, 0)),
            _resident(w_qkv.shape),
            _resident(w_z.shape),
            _resident(w_ab_split.shape),
            _resident(conv_w.shape),
            _resident((H2, 1)),
            _resident((H2, 1)),
        ],
        out_specs=[head_major, head_major, head_major, head_major,
                   pl.BlockSpec((1, 2 * H2, tile), lambda t: (t // tps, 0, t % tps)),
                   pl.BlockSpec((n_hg, tile, V7X_LANES), lambda t: (0, t, 0))],
        out_shape=[qkvz, qkvz, qkvz, qkvz, jax.ShapeDtypeStruct((B, 2 * H2, L), F32),
                   jax.ShapeDtypeStruct((n_hg, n_tok, V7X_LANES), F32)],
        compiler_params=_params(1),
    )(x, x, x, mod, w_qkv, w_z, w_ab_split, conv_w, a_log.reshape(H2, 1), dt_bias.reshape(H2, 1))


def _gdn_kernel(q_ref, k_ref, v_ref, z_ref, gcol_ref, grow_ref, s0f_ref, s0b_ref, ng_ref,
                o_ref, sf_ref, sb_ref,
                s_scr, wq_scr, u_scr, qk_scr, kd_scr, gl_scr, oacc_scr, *, group):
    Hb, L, dk = q_ref.shape[1], q_ref.shape[2], q_ref.shape[3]
    C = DN_CHUNK
    N = L // C
    n_groups = N // group
    gtok = group * C

    m = Hb * group
    for hh in range(Hb):
        s_scr[hh] = s0f_ref[0, hh]
        s_scr[Hb + hh] = s0b_ref[0, hh]
    oacc_scr[...] = jnp.zeros(oacc_scr.shape, F32)

    ii = lax.broadcasted_iota(jnp.int32, (C, C), 0)
    jj = lax.broadcasted_iota(jnp.int32, (C, C), 1)
    eye = (ii == jj).astype(F32)
    off_masks = []
    s = 1
    while s < C:
        same = (ii & -(2 * s)) == (jj & -(2 * s))
        lo = (same & ((ii & s) != 0) & ((jj & s) == 0)).astype(F32)
        up = (same & ((jj & s) != 0) & ((ii & s) == 0)).astype(F32)
        off_masks.append(jnp.stack([lo, up])[:, None])
        s *= 2

    def bmm(a, b):
        return jnp.einsum('nij,njd->nid', a, b, preferred_element_type=F32)

    def bmm_nt(a, b):
        return jnp.einsum('nid,njd->nij', a, b, preferred_element_type=F32)

    def heads(read):
        return jnp.concatenate([read(hh) for hh in range(Hb)], axis=0)

    def local(cgs, par):
        per_dir = []
        for d in range(2):
            tok = pl.ds(pl.multiple_of(cgs[d] * gtok, gtok), gtok)
            rows = pl.ds(pl.multiple_of(cgs[d] * group, group), group)
            q = heads(lambda hh: q_ref[0, hh, tok, :].reshape(group, C, dk))
            k = heads(lambda hh: k_ref[0, hh, tok, :].reshape(group, C, dk))
            v = heads(lambda hh: v_ref[0, hh, tok, :].reshape(group, C, dk))
            gcols = gcol_ref[0, tok, :]

            def col(kind):
                lane0 = kind * 2 * Hb + d * Hb
                return heads(lambda hh: gcols[:, lane0 + hh:lane0 + hh + 1].reshape(group, C, 1))

            gc, beta, gt = col(0), col(1), col(2)
            gc_row = heads(lambda hh: grow_ref[0, hh, d, rows, :])[:, None, :]
            beta_row = heads(lambda hh: grow_ref[0, hh, 2 + d, rows, :])[:, None, :]
            strict = (ii > jj) if d == 0 else (ii < jj)
            decay = jnp.exp(jnp.where(strict[None], gc - gc_row, -jnp.inf))
            a = bmm_nt(k, k) * beta * decay
            chains = slice(d * Hb, (d + 1) * Hb)
            qk_scr[par, chains] = (bmm_nt(q, k) * (decay + eye[None])).astype(BF16).reshape(Hb, group, C, C)
            q_dec = (q.astype(F32) * jnp.exp(gc)).astype(BF16)
            kd_scr[par, chains] = (k.astype(F32) * jnp.exp(gt - gc)).astype(BF16).reshape(Hb, group, C, dk)
            gl_scr[par, chains] = jnp.broadcast_to(jnp.exp(gt[:, 0:1, :]), (m, V7X_SUBLANES, dk)).reshape(
                Hb, group, V7X_SUBLANES, dk)
            per_dir.append((a, k, v, q_dec, gc_row, beta_row))
            yield
        a4 = jnp.stack([per_dir[0][0], per_dir[1][0]])
        t_inv = eye[None] - (a4 * off_masks[0]).reshape(2 * m, C, C)
        for mask in off_masks[1:]:
            t_b = t_inv.astype(BF16)
            a_off = (a4 * mask).reshape(2 * m, C, C).astype(BF16)
            t_inv = t_inv - bmm(t_b, bmm(a_off, t_b).astype(BF16))
            yield
        for d in range(2):
            _, k, v, q_dec, gc_row, beta_row = per_dir[d]
            chains = slice(d * Hb, (d + 1) * Hb)
            t_beta = t_inv[d * m:(d + 1) * m] * beta_row
            u_scr[par, chains] = bmm(t_beta.astype(BF16), v).reshape(Hb, group, C, dk)
            w = bmm((t_beta * jnp.exp(gc_row)).astype(BF16), k)
            wq_scr[par, chains] = jnp.concatenate([w.astype(BF16), q_dec], axis=1).reshape(Hb, group, 2 * C, dk)
            yield

    def step(cgs, par, i):
        slots = [(d, hh) for d in range(2) for hh in range(Hb)]
        ns = [i if d == 0 else group - 1 - i for d, _ in slots]
        states = [s_scr[c] for c in range(2 * Hb)]
        r = [jnp.dot(wq_scr[par, c, ns[c]], states[c].astype(BF16), preferred_element_type=F32)
             for c in range(2 * Hb)]
        v_new = [(u_scr[par, c, ns[c]] - r[c][:C]).astype(BF16) for c in range(2 * Hb)]
        o = [jnp.dot(qk_scr[par, c, ns[c]], v_new[c], preferred_element_type=F32) for c in range(2 * Hb)]
        upd = [lax.dot_general(kd_scr[par, c, ns[c]], v_new[c], (((0,), (0,)), ((), ())),
                               preferred_element_type=F32) for c in range(2 * Hb)]
        for c, (d, hh) in enumerate(slots):
            s_scr[c] = states[c] * gl_scr[par, c, ns[c]][0:1, :] + upd[c]
            rows = pl.ds(pl.multiple_of((cgs[d] * group + ns[c]) * C, C), C)
            oacc_scr[hh, rows, :] += r[c][C:] + o[c]

    span = min(n_groups, GDN_GROUPS_PER_ITERATION)
    group_of = lambda g: (g, n_groups - 1 - g)

    def run_span(it, _):
        g0 = it * span
        for _ in local(group_of(g0), 0):
            pass
        for j in range(span):
            ahead = local(group_of(g0 + j + 1), (j + 1) % 2) if j + 1 < span else iter(())
            for i in range(group):
                step(group_of(g0 + j), j % 2, i)
                next(ahead, None)
            for _ in ahead:
                pass
        return 0

    lax.fori_loop(0, n_groups // span, run_span, 0)

    for hh in range(Hb):
        sf_ref[0, hh] = s_scr[hh]
        sb_ref[0, hh] = s_scr[Hb + hh]
        o = oacc_scr[hh]
        o = o * lax.rsqrt(jnp.mean(o * o, axis=-1, keepdims=True) + 1e-6) * ng_ref[...]
        o_ref[0, :, hh * dk:(hh + 1) * dk] = (o * _silu(z_ref[0, hh].astype(F32))).astype(BF16)


def _gdn_core(q, k, v, z, gcol, grow, s0f, s0b, norm_g):
    B, H, L, dk = q.shape
    C = DN_CHUNK
    N = L // C
    group = min(N, GDN_GROUP_CHUNKS)
    Hb = H // gcol.shape[0]
    n_chains = 2 * Hb
    seq = pl.BlockSpec((1, Hb, L, dk), lambda b, h: (b, h, 0, 0))
    st = pl.BlockSpec((1, Hb, dk, dk), lambda b, h: (b, h, 0, 0))
    return pl.pallas_call(
        functools.partial(_gdn_kernel, group=group),
        grid=(B, H // Hb),
        in_specs=[
            seq, seq, seq, seq,
            pl.BlockSpec((1, L, V7X_LANES), lambda b, h: (h, b, 0)),
            pl.BlockSpec((1, Hb, 4, N, C), lambda b, h: (b, h, 0, 0, 0)),
            st, st,
            pl.BlockSpec((1, dk), lambda b, h: (0, 0)),
        ],
        out_specs=[pl.BlockSpec((1, L, Hb * dk), lambda b, h: (b, 0, h)), st, st],
        out_shape=[
            jax.ShapeDtypeStruct((B, L, H * dk), BF16),
            jax.ShapeDtypeStruct((B, H, dk, dk), F32),
            jax.ShapeDtypeStruct((B, H, dk, dk), F32),
        ],
        scratch_shapes=[
            pltpu.VMEM((n_chains, dk, dk), F32),
            pltpu.VMEM((2, n_chains, group, 2 * C, dk), BF16),
            pltpu.VMEM((2, n_chains, group, C, dk), F32),
            pltpu.VMEM((2, n_chains, group, C, C), BF16),
            pltpu.VMEM((2, n_chains, group, C, dk), BF16),
            pltpu.VMEM((2, n_chains, group, V7X_SUBLANES, dk), F32),
            pltpu.VMEM((Hb, L, dk), F32),
        ],
        compiler_params=_params(2),
    )(q, k, v, z, gcol, grow, s0f, s0b, norm_g.reshape(1, dk))


def _gdn_mixer(x_tok, B, mod, mod_row, w_qkv, w_z, w_ab, conv_w, a_log, dt_bias, norm_g, s0f, s0b, tile):
    n_tok = x_tok.shape[0]
    L = n_tok // B
    H, C = DN_HEADS, DN_CHUNK
    q, k, v, z, grows, gcol = _gdn_proj(x_tok, B, mod, mod_row, w_qkv, w_z, w_ab, conv_w, a_log, dt_bias, tile)
    grow = grows.reshape(B, 2, 2, H, L).transpose(0, 3, 1, 2, 4).reshape(B, H, 4, L // C, C)
    o, s_f, s_b = _gdn_core(q, k, v, z, gcol, grow, s0f, s0b, norm_g)
    return o.reshape(n_tok, -1), s_f, s_b


def kernel(x, c, ctx, c_ctx, w_mod, b_mod, ln_g, ln_b, pool_w, pool_scale, dn_w_in, dn_conv, dn_a_log, dn_dt_bias, dn_norm, dn_w_out, w_router, b_router, w_gate, w_up, w_down):
    B, L, D = x.shape
    Lc = ctx.shape[1]
    E = w_router.shape[1]
    F = w_gate.shape[-1]
    H = DN_HEADS
    dk = dn_norm.shape[-1]
    hk = H * dk
    ctx_row = B

    n_rows = -(-(B + 1) // V7X_SUBLANES) * V7X_SUBLANES
    cvecs = jnp.concatenate([c, c_ctx[None, :], jnp.zeros((n_rows - B - 1, D), F32)], axis=0)
    mods = _modulation(cvecs, w_mod, b_mod)
    ln = jnp.stack([ln_g, ln_b], axis=2).reshape(ln_g.shape[0], 4, D)
    w_router_hi = w_router.T.astype(BF16)
    w_router_t = jnp.stack([w_router_hi, (w_router.T - w_router_hi.astype(F32)).astype(BF16)])
    tile = TOKEN_TILE
    x_row = lambda t: t // (L // tile)
    c_row = lambda t: ctx_row

    mod = mods[0]
    w_gu = jnp.concatenate([w_gate[0], w_up[0]], axis=-1).astype(BF16)
    w_d = w_down[0].reshape(E * F, D).astype(BF16)
    streams = []
    for tok, rows, row_of_batch, row_of_tile in ((x, L // GRID_W, lambda b: b, x_row), (ctx, None, lambda b: ctx_row, c_row)):
        y = _pool_mixer(tok, mod, row_of_batch, pool_w[0], pool_scale[0], rows)
        streams.append(_sublayers(tok.reshape(-1, D), y.reshape(-1, D), None, mod, row_of_tile, ln[0], w_router_t,
                                  b_router, w_gu, w_d, tile))
    x_tok, ctx_tok = streams

    mod = mods[1]
    w_in = dn_w_in[0]
    w_qkv = w_in[:, :3 * hk].astype(BF16)
    w_z = w_in[:, 3 * hk:4 * hk].astype(BF16)
    w_ab = w_in[:, 4 * hk:]
    conv_w = jnp.concatenate([dn_conv[0], jnp.zeros((V7X_SUBLANES - dn_conv.shape[1], 3 * hk), F32)], axis=0)
    a_log, dt_bias, norm_g = dn_a_log[0], dn_dt_bias[0], dn_norm[0]
    zeros = jnp.zeros((B, H, dk, dk), F32)
    _, s_f, s_b = _gdn_mixer(ctx_tok, B, mod, c_row, w_qkv, w_z, w_ab, conv_w, a_log, dt_bias, norm_g, zeros, zeros,
                             min(tile, Lc))
    o, _, _ = _gdn_mixer(x_tok, B, mod, x_row, w_qkv, w_z, w_ab, conv_w, a_log, dt_bias, norm_g, s_f, s_b, tile)
    w_gu = jnp.concatenate([w_gate[1], w_up[1]], axis=-1).astype(BF16)
    w_d = w_down[1].reshape(E * F, D).astype(BF16)
    out = _sublayers(x_tok, o, dn_w_out[0].astype(BF16), mod, x_row, ln[1], w_router_t, b_router, w_gu, w_d, tile)
    return out.reshape(B, L, D)
```

```python
import functools

import numpy as np
import jax
import jax.numpy as jnp
from jax import lax
from jax.experimental import pallas as pl
from jax.experimental.pallas import tpu as pltpu

GRID_W = 64
POOL_WINDOWS = (2, 4, 8, 16)
DN_HEADS = 8
DN_CHUNK = 64
DN_CONV_TAPS = 5
N_GROUPS = 4
EXPERTS_PER_GROUP = 4
N_EXPERTS = N_GROUPS * EXPERTS_PER_GROUP
DEPTH = 2
DEEPNORM_ALPHA = (2.0 * DEPTH) ** 0.25
LN_EPS = 1e-5

V7X_LANES = 128
V7X_SUBLANES = 8
V7X_VMEM_LIMIT_BYTES = 56 * 1024 * 1024

TOKEN_TILE = 512
MOD_COLUMN_TILE = 1536
POOL_SLAB = 256
GDN_GROUP_CHUNKS = 8
GDN_GROUPS_PER_ITERATION = 4

F32 = jnp.float32
BF16 = jnp.bfloat16
HIGHEST = lax.Precision.HIGHEST


def _gdn_heads_per_step(seq_len):
    return DN_HEADS if seq_len <= GDN_GROUP_CHUNKS * DN_CHUNK else DN_HEADS // 2


def _params(n_grid_dims):
    return pltpu.CompilerParams(
        dimension_semantics=("arbitrary",) * n_grid_dims,
        vmem_limit_bytes=V7X_VMEM_LIMIT_BYTES,
    )


def _resident(shape):
    zeros = (0,) * len(shape)
    return pl.BlockSpec(shape, lambda *_: zeros, pipeline_mode=pl.Buffered(1))


def _silu(x):
    return x * jax.nn.sigmoid(x)


def _layer_norm(v, g, b):
    mu = jnp.mean(v, axis=-1, keepdims=True)
    cen = v - mu
    var = jnp.mean(cen * cen, axis=-1, keepdims=True)
    return cen * lax.rsqrt(var + LN_EPS) * g + b


def _mod_kernel(c_ref, w_ref, b_ref, o_ref):
    s = _silu(c_ref[...])
    o_ref[0] = jnp.dot(s, w_ref[0], precision=HIGHEST, preferred_element_type=F32) + b_ref[0]


def _modulation(cvecs, w_mod, b_mod):
    R, D = cvecs.shape
    depth, _, six_d = w_mod.shape
    tn = MOD_COLUMN_TILE
    out = pl.pallas_call(
        _mod_kernel,
        grid=(depth, six_d // tn),
        in_specs=[
            pl.BlockSpec((R, D), lambda i, n: (0, 0)),
            pl.BlockSpec((1, D, tn), lambda i, n: (i, 0, n)),
            pl.BlockSpec((1, 1, tn), lambda i, n: (i, 0, n)),
        ],
        out_specs=pl.BlockSpec((1, R, tn), lambda i, n: (i, 0, n)),
        out_shape=jax.ShapeDtypeStruct((depth, R, six_d), F32),
        compiler_params=_params(2),
    )(cvecs, w_mod, b_mod.reshape(depth, 1, six_d))
    return out.reshape(depth, R, 6, D)


def _pool_kernel(x_ref, mod_ref, band_ref, icnt_ref, w_ref, ps_ref, o_ref, *, grid_rows):
    L, Cg = x_ref.shape[1], x_ref.shape[2]
    slab = band_ref.shape[1]
    g = pl.program_id(1)

    def body(k):
        h = x_ref[0] * (1.0 + mod_ref[0, 1:2, :]) + mod_ref[0, 0:1, :]
        if grid_rows is None:
            s = h
        else:
            r3 = h.reshape(grid_rows, L // grid_rows, Cg)
            zpad = jnp.zeros((k // 2,) + r3.shape[1:], F32)
            win = jnp.concatenate([zpad, r3, zpad], axis=0)
            span = 1
            while span < k:
                win = win[:-span] + win[span:]
                span *= 2
            s = win[:grid_rows].reshape(L, Cg)
        band = band_ref[0]
        w = w_ref[0].astype(BF16)
        for i in range(L // slab):
            sl = slice(i * slab, (i + 1) * slab)
            s_i = s[sl]
            hi = s_i.astype(BF16)
            lo = (s_i - hi.astype(F32)).astype(BF16)
            tot = jnp.dot(band, hi, preferred_element_type=F32) + jnp.dot(band, lo, preferred_element_type=F32)
            ic = icnt_ref[0, sl, :]
            mean = tot * jnp.concatenate([ic] * (Cg // V7X_LANES), axis=-1)
            p = mean - h[sl]
            y = jnp.dot(p.astype(BF16), w, preferred_element_type=F32)
            o_ref[0, sl, :] = y * ps_ref[...]

    if grid_rows is None:
        body(None)
    else:
        for gi, k in enumerate(POOL_WINDOWS):
            pl.when(g == gi)(functools.partial(body, k))


def _pool_tables(L, grid_rows, slab):
    n_g = len(POOL_WINDOWS)
    band = np.zeros((n_g, slab, slab), np.float32)
    icnt = np.zeros((n_g, L), np.float32)
    t = np.arange(L)
    for gi, k in enumerate(POOL_WINDOWS):
        if grid_rows is None:
            pos, n, blk = np.arange(slab), L, np.zeros(slab, np.int64)
            cnt = np.clip(t + k - k // 2, 0, L) - np.clip(t - k // 2, 0, L)
        else:
            w = L // grid_rows
            pos, n, blk = np.arange(slab) % w, w, np.arange(slab) // w
            row, col = t // w, t % w
            cnt_r = np.clip(row + k - k // 2, 0, grid_rows) - np.clip(row - k // 2, 0, grid_rows)
            cnt_c = np.clip(col + k - k // 2, 0, w) - np.clip(col - k // 2, 0, w)
            cnt = cnt_r * cnt_c
        lo = np.clip(pos - k // 2, 0, n)
        hi = np.clip(pos + k - k // 2, 0, n)
        inside = (pos[None, :] >= lo[:, None]) & (pos[None, :] < hi[:, None]) & (blk[None, :] == blk[:, None])
        band[gi] = inside.astype(np.float32)
        icnt[gi] = 1.0 / cnt
    icnt = np.broadcast_to(icnt[:, :, None], (n_g, L, V7X_LANES))
    return jnp.asarray(band, BF16), jnp.asarray(icnt, F32)


def _pool_mixer(x, mod, mod_row, pool_w, pool_scale, grid_rows):
    B, L, D = x.shape
    n_g = len(POOL_WINDOWS)
    Cg = D // n_g
    slab = POOL_SLAB
    assert L == slab if grid_rows is None else (L % slab == 0 and slab % (L // grid_rows) == 0)
    band, icnt = _pool_tables(L, grid_rows, slab)
    return pl.pallas_call(
        functools.partial(_pool_kernel, grid_rows=grid_rows),
        grid=(B, n_g),
        in_specs=[
            pl.BlockSpec((1, L, Cg), lambda b, g: (b, 0, g)),
            pl.BlockSpec((1, 6, Cg), lambda b, g: (mod_row(b), 0, g)),
            pl.BlockSpec((1, slab, slab), lambda b, g: (g, 0, 0)),
            pl.BlockSpec((1, L, V7X_LANES), lambda b, g: (g, 0, 0)),
            pl.BlockSpec((1, Cg, Cg), lambda b, g: (g, 0, 0)),
            pl.BlockSpec((1, Cg), lambda b, g: (0, g)),
        ],
        out_specs=pl.BlockSpec((1, L, Cg), lambda b, g: (b, 0, g)),
        out_shape=jax.ShapeDtypeStruct((B, L, D), F32),
        compiler_params=_params(2),
    )(x, mod, band, icnt, pool_w, pool_scale.reshape(1, D))


def _route(scores, bias):
    sel = [s + b for s, b in zip(scores, bias)]
    E = EXPERTS_PER_GROUP
    grp = []
    for gi in range(N_GROUPS):
        v = sel[gi * E:(gi + 1) * E]
        best = None
        for a in range(E):
            for c in range(a + 1, E):
                pair = v[a] + v[c]
                best = pair if best is None else jnp.maximum(best, pair)
        grp.append(best)
    g_idx = jnp.zeros_like(grp[0], dtype=jnp.int32)
    g_best = grp[0]
    for gi in range(1, N_GROUPS):
        better = grp[gi] > g_best
        g_idx = jnp.where(better, gi, g_idx)
        g_best = jnp.where(better, grp[gi], g_best)
    in_sel, in_score = [], []
    for l in range(E):
        vs, vc = sel[l], scores[l]
        for gi in range(1, N_GROUPS):
            vs = jnp.where(g_idx == gi, sel[gi * E + l], vs)
            vc = jnp.where(g_idx == gi, scores[gi * E + l], vc)
        in_sel.append(vs)
        in_score.append(vc)
    i1 = jnp.zeros_like(g_idx)
    m1 = in_sel[0]
    for l in range(1, E):
        better = in_sel[l] > m1
        i1 = jnp.where(better, l, i1)
        m1 = jnp.where(better, in_sel[l], m1)
    i2 = jnp.full_like(g_idx, -1)
    m2 = jnp.full_like(m1, -jnp.inf)
    for l in range(E):
        better = (i1 != l) & ((in_sel[l] > m2) | (i2 < 0))
        i2 = jnp.where(better, l, i2)
        m2 = jnp.where(better, in_sel[l], m2)
    s1 = in_score[0]
    s2 = in_score[0]
    for l in range(1, E):
        s1 = jnp.where(i1 == l, in_score[l], s1)
        s2 = jnp.where(i2 == l, in_score[l], s2)
    tot = s1 + s2
    w1, w2 = s1 / tot, s2 / tot
    e1 = g_idx * E + i1
    e2 = g_idx * E + i2
    return [jnp.where(e1 == e, w1, 0.0) + jnp.where(e2 == e, w2, 0.0) for e in range(N_EXPERTS)]


def _sublayers_kernel(*refs, has_proj):
    if has_proj:
        x_ref, y_ref, wo_ref, mod_ref, ln_ref, wr_ref, br_ref, wgu_ref, wd_ref, o_ref, hid_ref, x1_ref, hb_ref = refs
    else:
        x_ref, y_ref, mod_ref, ln_ref, wr_ref, br_ref, wgu_ref, wd_ref, o_ref, hid_ref, x1_ref, hb_ref = refs
    T = x_ref.shape[0]
    F = wgu_ref.shape[2] // 2
    halves = (slice(0, T // 2), slice(T // 2, T))
    nt = (((1,), (1,)), ((), ()))
    n_early = 2
    logits, gus = [], [[] for _ in range(n_early)]
    y_full = jnp.dot(y_ref[...], wo_ref[...], preferred_element_type=F32) if has_proj else None
    for rs in halves:
        y = y_full[rs] if has_proj else y_ref[rs, :]
        v = DEEPNORM_ALPHA * x_ref[rs, :] + mod_ref[0, 2:3, :] * y
        x1 = _layer_norm(v, ln_ref[0:1, :], ln_ref[1:2, :])
        x1_ref[rs, :] = x1
        h = x1 * (1.0 + mod_ref[0, 4:5, :]) + mod_ref[0, 3:4, :]
        hb = h.astype(BF16)
        hb_ref[rs, :] = hb
        h_lo = (h - hb.astype(F32)).astype(BF16)
        logits.append(lax.dot_general(wr_ref[0], hb, nt, preferred_element_type=F32)
                      + lax.dot_general(wr_ref[1], hb, nt, preferred_element_type=F32)
                      + lax.dot_general(wr_ref[0], h_lo, nt, preferred_element_type=F32))
        for e in range(n_early):
            gus[e].append(jnp.dot(hb, wgu_ref[e], preferred_element_type=F32))
    logits = jnp.concatenate(logits, axis=1)
    hb = hb_ref[...]
    x1 = x1_ref[...]
    scores = jax.nn.sigmoid(logits)
    bias = br_ref[...]
    rows = _route([scores[e:e + 1, :] for e in range(N_EXPERTS)],
                  [jnp.broadcast_to(bias[e:e + 1, :], (1, T)) for e in range(N_EXPERTS)])
    cmb = jnp.concatenate(rows + [jnp.zeros((V7X_LANES - N_EXPERTS, T), F32)], axis=0).T
    for e in range(N_EXPERTS):
        if e < n_early:
            parts = zip(halves, gus[e])
        else:
            parts = [(slice(0, T), jnp.dot(hb, wgu_ref[e], preferred_element_type=F32))]
        for rs, gu in parts:
            hid = _silu(gu[:, :F]) * gu[:, F:] * cmb[rs, e:e + 1]
            hid_ref[rs, e * F:(e + 1) * F] = hid.astype(BF16)
    for rs in (slice(0, T // 2), slice(T // 2, T)):
        y2 = jnp.dot(hid_ref[rs, :], wd_ref[...], preferred_element_type=F32)
        v2 = DEEPNORM_ALPHA * x1[rs] + mod_ref[0, 5:6, :] * y2
        o_ref[rs, :] = _layer_norm(v2, ln_ref[2:3, :], ln_ref[3:4, :])


def _sublayers(x, y, w_out, mod, mod_row, ln, w_router_t, b_router, w_gu, w_d, tile):
    n_tok, D = x.shape
    E, _, F2 = w_gu.shape
    has_proj = w_out is not None
    tok = pl.BlockSpec((tile, D), lambda t: (t, 0))
    in_specs = [tok, tok]
    args = [x, y]
    if has_proj:
        in_specs.append(_resident(w_out.shape))
        args.append(w_out)
    in_specs += [
        pl.BlockSpec((1, 6, D), lambda t: (mod_row(t), 0, 0)),
        _resident(ln.shape),
        _resident(w_router_t.shape),
        _resident((E, 1)),
        _resident(w_gu.shape),
        _resident(w_d.shape),
    ]
    args += [mod, ln, w_router_t, b_router.reshape(E, 1), w_gu, w_d]
    return pl.pallas_call(
        functools.partial(_sublayers_kernel, has_proj=has_proj),
        grid=(n_tok // tile,),
        in_specs=in_specs,
        out_specs=tok,
        out_shape=jax.ShapeDtypeStruct((n_tok, D), F32),
        scratch_shapes=[
            pltpu.VMEM((tile, E * F2 // 2), BF16),
            pltpu.VMEM((tile, D), F32),
            pltpu.VMEM((tile, D), BF16),
        ],
        compiler_params=_params(1),
    )(*args)


def _proj_kernel(x_ref, xp_ref, xn_ref, mod_ref, wqkv_ref, wz_ref, wab_ref, cw_ref, alog_ref, dtb_ref,
                 q_ref, k_ref, v_ref, z_ref, grow_ref, gcol_ref, *, tiles_per_seq):
    T = x_ref.shape[0]
    H, dk = q_ref.shape[1], q_ref.shape[3]
    hk = H * dk
    H2 = alog_ref.shape[0]
    C = DN_CHUNK
    halo = xp_ref.shape[0]
    n_taps = DN_CONV_TAPS
    scale = 1.0 + mod_ref[0, 1:2, :]
    shift = mod_ref[0, 0:1, :]
    h = x_ref[...] * scale + shift
    pos = pl.program_id(0) % tiles_per_seq
    prev_ok = (pos != 0).astype(F32)
    next_ok = (pos != tiles_per_seq - 1).astype(F32)
    h_ext = jnp.concatenate([(xp_ref[...] * scale + shift) * prev_ok, h, (xn_ref[...] * scale + shift) * next_ok],
                            axis=0)
    u = jnp.dot(h_ext.astype(BF16), wqkv_ref[...], preferred_element_type=F32)
    acc = None
    for j in range(n_taps):
        d = j - n_taps // 2
        shifted = u if d == 0 else pltpu.roll(u, (-d) % (T + 2 * halo), axis=0)
        term = shifted[halo:halo + T] * cw_ref[j:j + 1, :]
        acc = term if acc is None else acc + term
    act = _silu(acc)
    for hd in range(H):
        qh = act[:, hd * dk:(hd + 1) * dk]
        kh = act[:, hk + hd * dk:hk + (hd + 1) * dk]
        q_ref[0, hd] = (qh * (lax.rsqrt(jnp.sum(qh * qh, axis=-1, keepdims=True) + 1e-6) * dk ** -0.5)).astype(BF16)
        k_ref[0, hd] = (kh * lax.rsqrt(jnp.sum(kh * kh, axis=-1, keepdims=True) + 1e-6)).astype(BF16)
        v_ref[0, hd] = act[:, 2 * hk + hd * dk:2 * hk + (hd + 1) * dk].astype(BF16)
    h_hi = h.astype(BF16)
    z = jnp.dot(h_hi, wz_ref[...], preferred_element_type=F32)
    for hd in range(H):
        z_ref[0, hd] = z[:, hd * dk:(hd + 1) * dk].astype(BF16)
    h_lo = (h - h_hi.astype(F32)).astype(BF16)
    nt = (((1,), (1,)), ((), ()))
    ab = (lax.dot_general(wab_ref[0], h_hi, nt, preferred_element_type=F32)
          + lax.dot_general(wab_ref[1], h_hi, nt, preferred_element_type=F32)
          + lax.dot_general(wab_ref[0], h_lo, nt, preferred_element_type=F32))
    a = ab[:H2] + dtb_ref[...]
    softplus = jnp.maximum(a, 0.0) + jnp.log1p(jnp.exp(-jnp.abs(a)))
    g = -jnp.exp(alog_ref[...]) * softplus
    beta = jax.nn.sigmoid(ab[H2:])
    pos = lax.broadcasted_iota(jnp.int32, (H2, T), 1) & (C - 1)
    pre = suf = g
    s = 1
    while s < C:
        pre = pre + jnp.where(pos >= s, pltpu.roll(pre, s, axis=1), 0.0)
        suf = suf + jnp.where(pos < C - s, pltpu.roll(suf, T - s, axis=1), 0.0)
        s *= 2
    tot = pre + suf - g
    fwd_rows = lax.broadcasted_iota(jnp.int32, (H2, T), 0) < H2 // 2
    gc = jnp.where(fwd_rows, pre, suf)
    grow_ref[0] = jnp.concatenate([gc, beta], axis=0)
    n_hg, Hb = gcol_ref.shape[0], H // gcol_ref.shape[0]
    for hg in range(n_hg):
        picks = [t[d * H + hg * Hb:d * H + (hg + 1) * Hb] for t in (gc, beta, tot) for d in range(2)]
        picks.append(jnp.zeros((V7X_LANES - 6 * Hb, T), F32))
        gcol_ref[hg] = jnp.concatenate(picks, axis=0).T


def _gdn_proj(x, B, mod, mod_row, w_qkv, w_z, w_ab, conv_w, a_log, dt_bias, tile):
    n_tok, D = x.shape
    L = n_tok // B
    H = DN_HEADS
    dk = w_z.shape[1] // H
    H2 = a_log.size
    n_hg = H // _gdn_heads_per_step(L)
    w_ab_t = w_ab.T
    w_ab_hi = w_ab_t.astype(BF16)
    w_ab_split = jnp.stack([w_ab_hi, (w_ab_t - w_ab_hi.astype(F32)).astype(BF16)])
    halo = V7X_SUBLANES
    tps = L // tile
    n_halo_blocks = n_tok // halo
    head_major = pl.BlockSpec((1, H, tile, dk), lambda t: (t // tps, 0, t % tps, 0))
    qkvz = jax.ShapeDtypeStruct((B, H, L, dk), BF16)
    return pl.pallas_call(
        functools.partial(_proj_kernel, tiles_per_seq=tps),
        grid=(n_tok // tile,),
        in_specs=[
            pl.BlockSpec((tile, D), lambda t: (t, 0)),
            pl.BlockSpec((halo, D), lambda t: (jnp.maximum(t * (tile // halo) - 1, 0), 0)),
            pl.BlockSpec((halo, D), lambda t: (jnp.minimum((t + 1) * (tile // halo), n_halo_blocks - 1), 0)),
            pl.BlockSpec((1, 6, D), lambda t: (mod_row(t), 0, 0)),
            _resident(w_qkv.shape),
            _resident(w_z.shape),
            _resident(w_ab_split.shape),
            _resident(conv_w.shape),
            _resident((H2, 1)),
            _resident((H2, 1)),
        ],
        out_specs=[head_major, head_major, head_major, head_major,
                   pl.BlockSpec((1, 2 * H2, tile), lambda t: (t // tps, 0, t % tps)),
                   pl.BlockSpec((n_hg, tile, V7X_LANES), lambda t: (0, t, 0))],
        out_shape=[qkvz, qkvz, qkvz, qkvz, jax.ShapeDtypeStruct((B, 2 * H2, L), F32),
                   jax.ShapeDtypeStruct((n_hg, n_tok, V7X_LANES), F32)],
        compiler_params=_params(1),
    )(x, x, x, mod, w_qkv, w_z, w_ab_split, conv_w, a_log.reshape(H2, 1), dt_bias.reshape(H2, 1))


def _gdn_kernel(q_ref, k_ref, v_ref, z_ref, gcol_ref, grow_ref, s0f_ref, s0b_ref, ng_ref,
                o_ref, sf_ref, sb_ref,
                s_scr, wq_scr, u_scr, qk_scr, kd_scr, gl_scr, oacc_scr, *, group):
    Hb, L, dk = q_ref.shape[1], q_ref.shape[2], q_ref.shape[3]
    C = DN_CHUNK
    N = L // C
    n_groups = N // group
    gtok = group * C

    m = Hb * group
    for hh in range(Hb):
        s_scr[hh] = s0f_ref[0, hh]
        s_scr[Hb + hh] = s0b_ref[0, hh]
    oacc_scr[...] = jnp.zeros(oacc_scr.shape, F32)

    ii = lax.broadcasted_iota(jnp.int32, (C, C), 0)
    jj = lax.broadcasted_iota(jnp.int32, (C, C), 1)
    eye = (ii == jj).astype(F32)
    off_masks = []
    s = 1
    while s < C:
        same = (ii & -(2 * s)) == (jj & -(2 * s))
        lo = (same & ((ii & s) != 0) & ((jj & s) == 0)).astype(F32)
        up = (same & ((jj & s) != 0) & ((ii & s) == 0)).astype(F32)
        off_masks.append(jnp.stack([lo, up])[:, None])
        s *= 2

    def bmm(a, b):
        return jnp.einsum('nij,njd->nid', a, b, preferred_element_type=F32)

    def bmm_nt(a, b):
        return jnp.einsum('nid,njd->nij', a, b, preferred_element_type=F32)

    def heads(read):
        return jnp.concatenate([read(hh) for hh in range(Hb)], axis=0)

    def local(cgs, par):
        per_dir = []
        for d in range(2):
            tok = pl.ds(pl.multiple_of(cgs[d] * gtok, gtok), gtok)
            rows = pl.ds(pl.multiple_of(cgs[d] * group, group), group)
            q = heads(lambda hh: q_ref[0, hh, tok, :].reshape(group, C, dk))
            k = heads(lambda hh: k_ref[0, hh, tok, :].reshape(group, C, dk))
            v = heads(lambda hh: v_ref[0, hh, tok, :].reshape(group, C, dk))
            gcols = gcol_ref[0, tok, :]

            def col(kind):
                lane0 = kind * 2 * Hb + d * Hb
                return heads(lambda hh: gcols[:, lane0 + hh:lane0 + hh + 1].reshape(group, C, 1))

            gc, beta, gt = col(0), col(1), col(2)
            gc_row = heads(lambda hh: grow_ref[0, hh, d, rows, :])[:, None, :]
            beta_row = heads(lambda hh: grow_ref[0, hh, 2 + d, rows, :])[:, None, :]
            strict = (ii > jj) if d == 0 else (ii < jj)
            decay = jnp.exp(jnp.where(strict[None], gc - gc_row, -jnp.inf))
            a = bmm_nt(k, k) * beta * decay
            chains = slice(d * Hb, (d + 1) * Hb)
            qk_scr[par, chains] = (bmm_nt(q, k) * (decay + eye[None])).astype(BF16).reshape(Hb, group, C, C)
            q_dec = (q.astype(F32) * jnp.exp(gc)).astype(BF16)
            kd_scr[par, chains] = (k.astype(F32) * jnp.exp(gt - gc)).astype(BF16).reshape(Hb, group, C, dk)
            gl_scr[par, chains] = jnp.broadcast_to(jnp.exp(gt[:, 0:1, :]), (m, V7X_SUBLANES, dk)).reshape(
                Hb, group, V7X_SUBLANES, dk)
            per_dir.append((a, k, v, q_dec, gc_row, beta_row))
            yield
        a4 = jnp.stack([per_dir[0][0], per_dir[1][0]])
        t_inv = eye[None] - (a4 * off_masks[0]).reshape(2 * m, C, C)
        for mask in off_masks[1:]:
            t_b = t_inv.astype(BF16)
            a_off = (a4 * mask).reshape(2 * m, C, C).astype(BF16)
            t_inv = t_inv - bmm(t_b, bmm(a_off, t_b).astype(BF16))
            yield
        for d in range(2):
            _, k, v, q_dec, gc_row, beta_row = per_dir[d]
            chains = slice(d * Hb, (d + 1) * Hb)
            t_beta = t_inv[d * m:(d + 1) * m] * beta_row
            u_scr[par, chains] = bmm(t_beta.astype(BF16), v).reshape(Hb, group, C, dk)
            w = bmm((t_beta * jnp.exp(gc_row)).astype(BF16), k)
            wq_scr[par, chains] = jnp.concatenate([w.astype(BF16), q_dec], axis=1).reshape(Hb, group, 2 * C, dk)
            yield

    def step(cgs, par, i):
        slots = [(d, hh) for d in range(2) for hh in range(Hb)]
        ns = [i if d == 0 else group - 1 - i for d, _ in slots]
        states = [s_scr[c] for c in range(2 * Hb)]
        r = [jnp.dot(wq_scr[par, c, ns[c]], states[c].astype(BF16), preferred_element_type=F32)
             for c in range(2 * Hb)]
        v_new = [(u_scr[par, c, ns[c]] - r[c][:C]).astype(BF16) for c in range(2 * Hb)]
        o = [jnp.dot(qk_scr[par, c, ns[c]], v_new[c], preferred_element_type=F32) for c in range(2 * Hb)]
        upd = [lax.dot_general(kd_scr[par, c, ns[c]], v_new[c], (((0,), (0,)), ((), ())),
                               preferred_element_type=F32) for c in range(2 * Hb)]
        for c, (d, hh) in enumerate(slots):
            s_scr[c] = states[c] * gl_scr[par, c, ns[c]][0:1, :] + upd[c]
            rows = pl.ds(pl.multiple_of((cgs[d] * group + ns[c]) * C, C), C)
            oacc_scr[hh, rows, :] += r[c][C:] + o[c]

    span = min(n_groups, GDN_GROUPS_PER_ITERATION)
    group_of = lambda g: (g, n_groups - 1 - g)

    def run_span(it, _):
        g0 = it * span
        for _ in local(group_of(g0), 0):
            pass
        for j in range(span):
            ahead = local(group_of(g0 + j + 1), (j + 1) % 2) if j + 1 < span else iter(())
            for i in range(group):
                step(group_of(g0 + j), j % 2, i)
                next(ahead, None)
            for _ in ahead:
                pass
        return 0

    lax.fori_loop(0, n_groups // span, run_span, 0)

    for hh in range(Hb):
        sf_ref[0, hh] = s_scr[hh]
        sb_ref[0, hh] = s_scr[Hb + hh]
        o = oacc_scr[hh]
        o = o * lax.rsqrt(jnp.mean(o * o, axis=-1, keepdims=True) + 1e-6) * ng_ref[...]
        o_ref[0, :, hh * dk:(hh + 1) * dk] = (o * _silu(z_ref[0, hh].astype(F32))).astype(BF16)


def _gdn_core(q, k, v, z, gcol, grow, s0f, s0b, norm_g):
    B, H, L, dk = q.shape
    C = DN_CHUNK
    N = L // C
    group = min(N, GDN_GROUP_CHUNKS)
    Hb = H // gcol.shape[0]
    n_chains = 2 * Hb
    seq = pl.BlockSpec((1, Hb, L, dk), lambda b, h: (b, h, 0, 0))
    st = pl.BlockSpec((1, Hb, dk, dk), lambda b, h: (b, h, 0, 0))
    return pl.pallas_call(
        functools.partial(_gdn_kernel, group=group),
        grid=(B, H // Hb),
        in_specs=[
            seq, seq, seq, seq,
            pl.BlockSpec((1, L, V7X_LANES), lambda b, h: (h, b, 0)),
            pl.BlockSpec((1, Hb, 4, N, C), lambda b, h: (b, h, 0, 0, 0)),
            st, st,
            pl.BlockSpec((1, dk), lambda b, h: (0, 0)),
        ],
        out_specs=[pl.BlockSpec((1, L, Hb * dk), lambda b, h: (b, 0, h)), st, st],
        out_shape=[
            jax.ShapeDtypeStruct((B, L, H * dk), BF16),
            jax.ShapeDtypeStruct((B, H, dk, dk), F32),
            jax.ShapeDtypeStruct((B, H, dk, dk), F32),
        ],
        scratch_shapes=[
            pltpu.VMEM((n_chains, dk, dk), F32),
            pltpu.VMEM((2, n_chains, group, 2 * C, dk), BF16),
            pltpu.VMEM((2, n_chains, group, C, dk), F32),
            pltpu.VMEM((2, n_chains, group, C, C), BF16),
            pltpu.VMEM((2, n_chains, group, C, dk), BF16),
            pltpu.VMEM((2, n_chains, group, V7X_SUBLANES, dk), F32),
            pltpu.VMEM((Hb, L, dk), F32),
        ],
        compiler_params=_params(2),
    )(q, k, v, z, gcol, grow, s0f, s0b, norm_g.reshape(1, dk))


def _gdn_mixer(x_tok, B, mod, mod_row, w_qkv, w_z, w_ab, conv_w, a_log, dt_bias, norm_g, s0f, s0b, tile):
    n_tok = x_tok.shape[0]
    L = n_tok // B
    H, C = DN_HEADS, DN_CHUNK
    q, k, v, z, grows, gcol = _gdn_proj(x_tok, B, mod, mod_row, w_qkv, w_z, w_ab, conv_w, a_log, dt_bias, tile)
    grow = grows.reshape(B, 2, 2, H, L).transpose(0, 3, 1, 2, 4).reshape(B, H, 4, L // C, C)
    o, s_f, s_b = _gdn_core(q, k, v, z, gcol, grow, s0f, s0b, norm_g)
    return o.reshape(n_tok, -1), s_f, s_b


def kernel(x, c, ctx, c_ctx, w_mod, b_mod, ln_g, ln_b, pool_w, pool_scale, dn_w_in, dn_conv, dn_a_log, dn_dt_bias, dn_norm, dn_w_out, w_router, b_router, w_gate, w_up, w_down):
    B, L, D = x.shape
    Lc = ctx.shape[1]
    E = w_router.shape[1]
    F = w_gate.shape[-1]
    H = DN_HEADS
    dk = dn_norm.shape[-1]
    hk = H * dk
    ctx_row = B

    n_rows = -(-(B + 1) // V7X_SUBLANES) * V7X_SUBLANES
    cvecs = jnp.concatenate([c, c_ctx[None, :], jnp.zeros((n_rows - B - 1, D), F32)], axis=0)
    mods = _modulation(cvecs, w_mod, b_mod)
    ln = jnp.stack([ln_g, ln_b], axis=2).reshape(ln_g.shape[0], 4, D)
    w_router_hi = w_router.T.astype(BF16)
    w_router_t = jnp.stack([w_router_hi, (w_router.T - w_router_hi.astype(F32)).astype(BF16)])
    tile = TOKEN_TILE
    x_row = lambda t: t // (L // tile)
    c_row = lambda t: ctx_row

    mod = mods[0]
    w_gu = jnp.concatenate([w_gate[0], w_up[0]], axis=-1).astype(BF16)
    w_d = w_down[0].reshape(E * F, D).astype(BF16)
    streams = []
    for tok, rows, row_of_batch, row_of_tile in ((x, L // GRID_W, lambda b: b, x_row), (ctx, None, lambda b: ctx_row, c_row)):
        y = _pool_mixer(tok, mod, row_of_batch, pool_w[0], pool_scale[0], rows)
        streams.append(_sublayers(tok.reshape(-1, D), y.reshape(-1, D), None, mod, row_of_tile, ln[0], w_router_t,
                                  b_router, w_gu, w_d, tile))
    x_tok, ctx_tok = streams

    mod = mods[1]
    w_in = dn_w_in[0]
    w_qkv = w_in[:, :3 * hk].astype(BF16)
    w_z = w_in[:, 3 * hk:4 * hk].astype(BF16)
    w_ab = w_in[:, 4 * hk:]
    conv_w = jnp.concatenate([dn_conv[0], jnp.zeros((V7X_SUBLANES - dn_conv.shape[1], 3 * hk), F32)], axis=0)
    a_log, dt_bias, norm_g = dn_a_log[0], dn_dt_bias[0], dn_norm[0]
    zeros = jnp.zeros((B, H, dk, dk), F32)
    _, s_f, s_b = _gdn_mixer(ctx_tok, B, mod, c_row, w_qkv, w_z, w_ab, conv_w, a_log, dt_bias, norm_g, zeros, zeros,
                             min(tile, Lc))
    o, _, _ = _gdn_mixer(x_tok, B, mod, x_row, w_qkv, w_z, w_ab, conv_w, a_log, dt_bias, norm_g, s_f, s_b, tile)
    w_gu = jnp.concatenate([w_gate[1], w_up[1]], axis=-1).astype(BF16)
    w_d = w_down[1].reshape(E * F, D).astype(BF16)
    out = _sublayers(x_tok, o, dn_w_out[0].astype(BF16), mod, x_row, ln[1], w_router_t, b_router, w_gu, w_d, tile)
    return out.reshape(B, L, D)
```
